```python
import jax, jax.numpy as jnp
from jax import lax
import numpy as np

D_MODEL = 2048
BATCH = 8
SEQ = 2048
DEPTH = 2

M_HEADS = 4
M_QK = 128
M_V = 256
M_QK_W = M_HEADS * M_QK
M_V_W = M_HEADS * M_V
M_CHUNK = 128
MLA_HEADS = 8
NOPE_DIM = 128
ROPE_DIM = 64
QK_DIM = NOPE_DIM + ROPE_DIM
MLA_V = 128
MLA_W = MLA_HEADS * MLA_V
Q_RANK = 512
KV_RANK = 256
ROPE_THETA = 10000.0
Q_BLOCK = 128
POOL_WINDOWS = (2, 4, 8, 16)
POOL_GROUPS = 4
POOL_GROUP_W = 256
POOL_W = POOL_GROUPS * POOL_GROUP_W
N_BRANCH = 3
BRANCH_W = 1024
FFN_DIM = ((8 * D_MODEL // 3 + 255) // 256) * 256
NORM_EPS = 1e-6
IN_SIZES = (M_QK_W, M_QK_W, M_V_W, M_V_W, M_HEADS, M_HEADS, Q_RANK, KV_RANK, ROPE_DIM, POOL_W, N_BRANCH * D_MODEL)
IN_W = sum(IN_SIZES)

kernel_name = "hybrid_mlstm_mla_pool_adaln"


def rmsnorm(x, g):
    xf = x.astype(jnp.float32)
    y = xf * lax.rsqrt(jnp.mean(xf * xf, axis=-1, keepdims=True) + NORM_EPS)
    return (y * g.astype(jnp.float32)).astype(x.dtype)


def modulate(h, shift, scale):
    return h * (1 + scale[:, None, :]) + shift[:, None, :]


def split_cols(z, sizes):
    out = []
    off = 0
    for s in sizes:
        out.append(z[..., off:off + s])
        off += s
    return out


def rope(x):
    S, R = x.shape[1], x.shape[-1]
    pos = jnp.arange(S, dtype=jnp.float32)
    freqs = ROPE_THETA ** (-jnp.arange(0, R, 2, dtype=jnp.float32) / R)
    ang = pos[:, None] * freqs[None, :]
    cos = jnp.cos(ang)[None, :, None, :].astype(x.dtype)
    sin = jnp.sin(ang)[None, :, None, :].astype(x.dtype)
    x1, x2 = x[..., : R // 2], x[..., R // 2:]
    return jnp.concatenate([x1 * cos - x2 * sin, x1 * sin + x2 * cos], axis=-1)


def causal_attention(q, k, v):
    S, Dh = q.shape[1], q.shape[-1]
    scale = Dh ** -0.5
    outs = []
    for blk in range(S // Q_BLOCK):
        s0, s1 = blk * Q_BLOCK, (blk + 1) * Q_BLOCK
        logits = jnp.einsum('bqhd,bkhd->bhqk', q[:, s0:s1], k[:, :s1],
                            preferred_element_type=jnp.float32) * scale
        mask = (s0 + jnp.arange(Q_BLOCK))[:, None] >= jnp.arange(s1)[None, :]
        logits = jnp.where(mask, logits, -jnp.inf)
        p = jax.nn.softmax(logits, axis=-1).astype(v.dtype)
        outs.append(jnp.einsum('bhqk,bkhv->bqhv', p, v[:, :s1]))
    return jnp.concatenate(outs, axis=1)


def mlstm_chunkwise(q, k, v, i_pre, f_pre):
    f32 = jnp.float32
    B, S, H, DK = q.shape
    DV = v.shape[-1]
    L = M_CHUNK
    NC = S // L
    q = q.astype(f32)
    k = k.astype(f32) * (DK ** -0.5)
    v = v.astype(f32)
    ig = i_pre.astype(f32)
    logf = jax.nn.log_sigmoid(f_pre.astype(f32))

    def to_chunks(a):
        return jnp.moveaxis(a.reshape((B, NC, L) + a.shape[2:]), (1, 3), (0, 2))

    qc, kc, vc = to_chunks(q), to_chunks(k), to_chunks(v)
    ic = to_chunks(ig)
    bc = jnp.cumsum(to_chunks(logf), axis=-1)
    tril = jnp.tril(jnp.ones((L, L), dtype=bool))

    def step(carry, inp):
        C, n, m = carry
        qb, kb, vb, ib, bb = inp
        a = bb + m[..., None]
        Dm = bb[..., :, None] - bb[..., None, :] + ib[..., None, :]
        Dm = jnp.where(tril, Dm, -jnp.inf)
        m_t = jnp.maximum(a, jnp.max(Dm, axis=-1))
        w_inter = jnp.exp(a - m_t)
        s = jnp.einsum('bhtd,bhsd->bhts', qb, kb) * jnp.exp(Dm - m_t[..., None])
        num = (w_inter[..., None] * jnp.einsum('bhvd,bhtd->bhtv', C, qb)
               + jnp.einsum('bhts,bhsv->bhtv', s, vb))
        den = w_inter * jnp.einsum('bhd,bhtd->bht', n, qb) + jnp.sum(s, axis=-1)
        h = num / jnp.maximum(jnp.abs(den), jnp.exp(-m_t))[..., None]
        bL = bb[..., -1]
        g = bL[..., None] - bb + ib
        m_new = jnp.maximum(bL + m, jnp.max(g, axis=-1))
        decay = jnp.exp(bL + m - m_new)
        ws = jnp.exp(g - m_new[..., None])
        C_new = decay[..., None, None] * C + jnp.einsum('bhs,bhsv,bhsd->bhvd', ws, vb, kb)
        n_new = decay[..., None] * n + jnp.einsum('bhs,bhsd->bhd', ws, kb)
        return (C_new, n_new, m_new), h

    init = (jnp.zeros((B, H, DV, DK), f32), jnp.zeros((B, H, DK), f32), jnp.zeros((B, H), f32))
    _, hs = lax.scan(step, init, (qc, kc, vc, ic, bc))
    return jnp.moveaxis(hs, (0, 2), (1, 3)).reshape(B, S, H, DV)


def pool_mixer(u, w_pool, b_pool, s_pool):
    B, S, _ = u.shape
    uf = u.astype(jnp.float32).reshape(B, S, POOL_GROUPS, POOL_GROUP_W)
    cs = jnp.cumsum(uf, axis=1)
    t = jnp.arange(S)
    outs = []
    for g, w in enumerate(POOL_WINDOWS):
        csg = cs[:, :, g]
        lag = jnp.pad(csg, ((0, 0), (w, 0), (0, 0)))[:, :S]
        cnt = jnp.minimum(t + 1, w).astype(jnp.float32)[None, :, None]
        outs.append((csg - lag) / cnt - uf[:, :, g])
    pooled = jnp.stack(outs, axis=2).astype(u.dtype)
    y = jnp.einsum('bsgc,gce->bsge', pooled, w_pool) + b_pool
    return y.reshape(B, S, POOL_W) * s_pool


def setup_inputs(seed: int = 0) -> dict:
    key = jax.random.key(seed)
    ks = jax.random.split(key, 24)
    f32 = jnp.float32

    def dense(k, shape, fan_in):
        return jax.random.normal(k, shape, f32) * (fan_in ** -0.5)

    def gain(k, shape):
        return 1.0 + 0.1 * jax.random.normal(k, shape, f32)

    kb = jax.random.split(ks[6], 2)
    b_mgate = jnp.stack([0.1 * jax.random.normal(kb[0], (DEPTH, M_HEADS), f32),
                         3.0 + 0.5 * jax.random.normal(kb[1], (DEPTH, M_HEADS), f32)], axis=1)
    return {
        "x": jax.random.normal(ks[0], (BATCH, SEQ, D_MODEL), f32),
        "c": jax.random.normal(ks[1], (BATCH, D_MODEL), f32),
        "w_ada": dense(ks[2], (DEPTH, D_MODEL, 6 * D_MODEL), D_MODEL),
        "b_ada": 0.02 * jax.random.normal(ks[3], (DEPTH, 6 * D_MODEL), f32),
        "g_norm1": gain(ks[4], (DEPTH, D_MODEL)),
        "w_in": dense(ks[5], (DEPTH, D_MODEL, IN_W), D_MODEL),
        "b_mgate": b_mgate,
        "g_mnorm": gain(ks[7], (DEPTH, M_HEADS, M_V)),
        "g_qlat": gain(ks[8], (DEPTH, Q_RANK)),
        "w_uq": dense(ks[9], (DEPTH, Q_RANK, MLA_HEADS * QK_DIM), Q_RANK),
        "g_kvlat": gain(ks[10], (DEPTH, KV_RANK)),
        "w_ukv": dense(ks[11], (DEPTH, KV_RANK, MLA_HEADS * (NOPE_DIM + MLA_V)), KV_RANK),
        "g_qn": gain(ks[12], (DEPTH, QK_DIM)),
        "g_kn": gain(ks[13], (DEPTH, QK_DIM)),
        "w_pool": dense(ks[14], (DEPTH, POOL_GROUPS, POOL_GROUP_W, POOL_GROUP_W), POOL_GROUP_W),
        "b_pool": 0.02 * jax.random.normal(ks[15], (DEPTH, POOL_GROUPS, POOL_GROUP_W), f32),
        "s_pool": gain(ks[16], (DEPTH, POOL_W)),
        "w_branch": dense(ks[17], (DEPTH, N_BRANCH, BRANCH_W, D_MODEL), BRANCH_W),
        "w_out": dense(ks[18], (DEPTH, D_MODEL, D_MODEL), D_MODEL),
        "g_norm2": gain(ks[19], (DEPTH, D_MODEL)),
        "w_ffn_in": dense(ks[20], (DEPTH, D_MODEL, 2 * FFN_DIM), D_MODEL),
        "w_ffn_out": dense(ks[21], (DEPTH, FFN_DIM, D_MODEL), FFN_DIM),
    }


def reference(x, c, w_ada, b_ada, g_norm1, w_in, b_mgate, g_mnorm, g_qlat, w_uq, g_kvlat, w_ukv,
              g_qn, g_kn, w_pool, b_pool, s_pool, w_branch, w_out, g_norm2, w_ffn_in, w_ffn_out):
    B, S, D = x.shape
    c_act = jax.nn.silu(c)
    for l in range(DEPTH):
        mod = c_act @ w_ada[l] + b_ada[l]
        shift1, scale1, gate1, shift2, scale2, gate2 = jnp.split(mod, 6, axis=-1)

        h = modulate(rmsnorm(x, g_norm1[l]), shift1, scale1)
        z = h @ w_in[l]
        q_m, k_m, v_m, o_m, i_m, f_m, q_lat, kv_lat, k_r, u_pool, gates = split_cols(z, IN_SIZES)

        hm = mlstm_chunkwise(q_m.reshape(B, S, M_HEADS, M_QK), k_m.reshape(B, S, M_HEADS, M_QK),
                             v_m.reshape(B, S, M_HEADS, M_V),
                             i_m + b_mgate[l, 0], f_m + b_mgate[l, 1]).astype(x.dtype)
        hm = rmsnorm(hm, g_mnorm[l]) * jax.nn.sigmoid(o_m.reshape(B, S, M_HEADS, M_V))
        br_a = hm.reshape(B, S, M_V_W)

        q = (rmsnorm(q_lat, g_qlat[l]) @ w_uq[l]).reshape(B, S, MLA_HEADS, QK_DIM)
        kv = (rmsnorm(kv_lat, g_kvlat[l]) @ w_ukv[l]).reshape(B, S, MLA_HEADS, NOPE_DIM + MLA_V)
        k_nope, v = kv[..., :NOPE_DIM], kv[..., NOPE_DIM:]
        k = jnp.concatenate([k_nope, jnp.broadcast_to(k_r[:, :, None, :], (B, S, MLA_HEADS, ROPE_DIM))], axis=-1)
        q = rmsnorm(q, g_qn[l])
        k = rmsnorm(k, g_kn[l])
        q = jnp.concatenate([q[..., :NOPE_DIM], rope(q[..., NOPE_DIM:])], axis=-1)
        k = jnp.concatenate([k[..., :NOPE_DIM], rope(k[..., NOPE_DIM:])], axis=-1)
        br_b = causal_attention(q, k, v).reshape(B, S, MLA_W)

        br_c = pool_mixer(u_pool, w_pool[l], b_pool[l], s_pool[l])

        g = jax.nn.sigmoid(gates).reshape(B, S, N_BRANCH, D)
        merged = (g[:, :, 0] * (br_a @ w_branch[l, 0])
                  + g[:, :, 1] * (br_b @ w_branch[l, 1])
                  + g[:, :, 2] * (br_c @ w_branch[l, 2]))
        x = x + gate1[:, None, :] * (merged @ w_out[l])

        h2 = modulate(rmsnorm(x, g_norm2[l]), shift2, scale2)
        gu = h2 @ w_ffn_in[l]
        ff = (jax.nn.silu(gu[..., :FFN_DIM]) * gu[..., FFN_DIM:]) @ w_ffn_out[l]
        x = x + gate2[:, None, :] * ff
    return x
```

```python
import functools

import jax
import jax.numpy as jnp
from jax import lax
from jax.experimental import pallas as pl
from jax.experimental.pallas import tpu as pltpu

F32 = jnp.float32
BF16 = jnp.bfloat16

D_MODEL = 2048
M_HEADS = 4
M_QK = 128
M_V = 256
M_QK_W = M_HEADS * M_QK
M_V_W = M_HEADS * M_V
M_CHUNK = 128
MLA_HEADS = 8
NOPE_DIM = 128
ROPE_DIM = 64
ROPE_HALF = ROPE_DIM // 2
QK_DIM = NOPE_DIM + ROPE_DIM
MLA_V = 128
MLA_W = MLA_HEADS * MLA_V
Q_RANK = 512
KV_RANK = 256
ROPE_THETA = 10000.0
POOL_WINDOWS = (2, 4, 8, 16)
POOL_GROUPS = 4
POOL_GROUP_W = 256
POOL_W = POOL_GROUPS * POOL_GROUP_W
N_BRANCH = 3
BRANCH_W = 1024
FFN_DIM = ((8 * D_MODEL // 3 + 255) // 256) * 256
NORM_EPS = 1e-6
IN_SIZES = (M_QK_W, M_QK_W, M_V_W, M_V_W, M_HEADS, M_HEADS, Q_RANK, KV_RANK, ROPE_DIM, POOL_W,
            N_BRANCH * D_MODEL)

LANES = 128
QK_PAD = 2 * LANES
GATE_ROWS = 16

Z_QM = 0
Z_KM = Z_QM + M_QK_W
Z_VM = Z_KM + M_QK_W
Z_OM = Z_VM + M_V_W
Z_QLAT = Z_OM + M_V_W
Z_KVLAT = Z_QLAT + Q_RANK
Z_KR = Z_KVLAT + KV_RANK
Z_POOL = Z_KR + 2 * LANES
Z_GATES = Z_POOL + POOL_W
Z_W = Z_GATES + N_BRANCH * D_MODEL

V7X_VMEM_LIMIT = 52 * 1024 * 1024

TM_IN = 1024
TN_IN = 512
NORM_ROWS = 256
TM_MLA = 512
TQ = 256
TM_MERGE = 512
TN_MERGE = 512
TM_FFN = 512
TF_FFN = 512
TN_ADA = 1024


def _params(semantics):
    return pltpu.CompilerParams(dimension_semantics=semantics, vmem_limit_bytes=V7X_VMEM_LIMIT)


def _rms(x, width=None):
    ss = jnp.sum(x * x, axis=-1, keepdims=True)
    n = x.shape[-1] if width is None else width
    return x * lax.rsqrt(ss * (1.0 / n) + NORM_EPS)


def _dot(a, b):
    return jnp.dot(a, b, preferred_element_type=F32)


def _dot_nt(a, b):
    return lax.dot_general(a, b, (((1,), (1,)), ((), ())), preferred_element_type=F32)


def _ada_kernel(c_ref, w_ref, b_ref, o_ref):
    c = c_ref[...]
    ca = (c * jax.nn.sigmoid(c)).astype(BF16)
    o_ref[0] = _dot(ca, w_ref[0].astype(BF16)) + b_ref[0]


def _ada(c, w_ada, b_ada):
    depth, d, n = w_ada.shape
    b = c.shape[0]
    return pl.pallas_call(
        _ada_kernel,
        grid=(depth, n // TN_ADA),
        in_specs=[
            pl.BlockSpec((b, d), lambda l, j: (0, 0)),
            pl.BlockSpec((1, d, TN_ADA), lambda l, j: (l, 0, j)),
            pl.BlockSpec((1, 1, TN_ADA), lambda l, j: (l, 0, j)),
        ],
        out_specs=pl.BlockSpec((1, b, TN_ADA), lambda l, j: (l, 0, j)),
        out_shape=jax.ShapeDtypeStruct((depth, b, n), F32),
        compiler_params=_params(("parallel", "parallel")),
        name="adaln",
    )(c, w_ada, b_ada.reshape(depth, 1, n))


def _norm_modulate_into(x_ref, g_ref, sc_ref, sh_ref, h_scr):
    g = g_ref[...]
    sc = 1.0 + sc_ref[0]
    sh = sh_ref[0]

    def body(r, carry):
        r0 = pl.multiple_of(r * NORM_ROWS, NORM_ROWS)
        x = x_ref[pl.ds(r0, NORM_ROWS), :]
        h_scr[pl.ds(r0, NORM_ROWS), :] = ((_rms(x) * g) * sc + sh).astype(BF16)
        return carry

    lax.fori_loop(0, x_ref.shape[0] // NORM_ROWS, body, 0)


def _inproj_kernel(x_ref, g_ref, sc_ref, sh_ref, w_ref, wif_ref, z_ref, gt_ref, h_scr):
    @pl.when(pl.program_id(1) == 0)
    def _():
        _norm_modulate_into(x_ref, g_ref, sc_ref, sh_ref, h_scr)
        gt_ref[...] = _dot_nt(wif_ref[...], h_scr[...])

    z_ref[...] = _dot(h_scr[...], w_ref[...]).astype(BF16)


def _inproj(x2, g, scale, shift, w_z, w_if, seq):
    t, d = x2.shape
    per_b = seq // TM_IN
    vec = pl.BlockSpec((1, 1, d), lambda i, j: (i // per_b, 0, 0))
    return pl.pallas_call(
        _inproj_kernel,
        grid=(t // TM_IN, Z_W // TN_IN),
        in_specs=[
            pl.BlockSpec((TM_IN, d), lambda i, j: (i, 0)),
            pl.BlockSpec((1, d), lambda i, j: (0, 0)),
            vec, vec,
            pl.BlockSpec((d, TN_IN), lambda i, j: (0, j)),
            pl.BlockSpec((GATE_ROWS, d), lambda i, j: (0, 0)),
        ],
        out_specs=[
            pl.BlockSpec((TM_IN, TN_IN), lambda i, j: (i, j)),
            pl.BlockSpec((GATE_ROWS, TM_IN), lambda i, j: (0, i)),
        ],
        out_shape=[
            jax.ShapeDtypeStruct((t, Z_W), BF16),
            jax.ShapeDtypeStruct((GATE_ROWS, t), F32),
        ],
        scratch_shapes=[pltpu.VMEM((TM_IN, d), BF16)],
        compiler_params=_params(("parallel", "arbitrary")),
        name="inproj",
    )(x2, g.reshape(1, d), scale, shift, w_z, w_if)


def _mlstm_kernel(bias_ref, q_ref, k_ref, v_ref, o_ref, ig_ref, fg_ref, gn_ref, out_ref,
                  b_scr, i_scr):
    L = M_CHUNK
    nc = ig_ref.shape[2]
    head = pl.program_id(1)
    scale = M_QK ** -0.5

    ig = ig_ref[0, 0] + bias_ref[0, head]
    fg = fg_ref[0, 0] + bias_ref[1, head]
    logf = jnp.minimum(fg, 0.0) - jnp.log(1.0 + jnp.exp(-jnp.abs(fg)))
    lane = lax.broadcasted_iota(jnp.int32, (nc, L), 1)
    bc = logf
    sh = 1
    while sh < L:
        bc = bc + jnp.where(lane >= sh, pltpu.roll(bc, sh, axis=1), 0.0)
        sh *= 2
    b_scr[...] = bc
    i_scr[...] = ig

    rows = lax.broadcasted_iota(jnp.int32, (L, L), 0)
    cols = lax.broadcasted_iota(jnp.int32, (L, L), 1)
    eye = rows == cols
    tril = rows >= cols

    def to_col(row):
        return jnp.sum(jnp.where(eye, jnp.broadcast_to(row, (L, L)), 0.0), axis=1, keepdims=True)

    gn = gn_ref[0]

    def step(c, carry):
        ct, n, m = carry
        r0 = pl.multiple_of(c * L, L)
        qb = q_ref[0, pl.ds(r0, L), :]
        kb = k_ref[0, pl.ds(r0, L), :]
        vb = v_ref[0, pl.ds(r0, L), :]
        b_row = b_scr[pl.ds(c, 1), :]
        i_row = i_scr[pl.ds(c, 1), :]
        b_col = to_col(b_row)
        i_col = to_col(i_row)

        a_col = b_col + m
        dm = jnp.where(tril, b_col - b_row + i_row, -jnp.inf)
        m_t = jnp.maximum(a_col, jnp.max(dm, axis=1, keepdims=True))
        w_inter = jnp.exp(a_col - m_t)
        s = (_dot_nt(qb, kb) * scale) * jnp.exp(dm - m_t)
        num = w_inter * _dot(qb, ct.astype(BF16)) + _dot(s.astype(BF16), vb)
        den = (w_inter * jnp.sum(qb.astype(F32) * n, axis=1, keepdims=True)
               + jnp.sum(s, axis=1, keepdims=True))
        hh = num * (1.0 / jnp.maximum(jnp.abs(den), jnp.exp(-m_t)))

        o = o_ref[0, pl.ds(r0, L), :].astype(F32)
        out_ref[0, pl.ds(r0, L), :] = ((_rms(hh) * gn) * jax.nn.sigmoid(o)).astype(BF16)

        b_last = b_row[:, L - 1:L]
        g_row = b_last - b_row + i_row
        m_new = jnp.maximum(b_last + m, jnp.max(g_row, axis=1, keepdims=True))
        decay = jnp.exp(b_last + m - m_new)
        kw = kb.astype(F32) * (jnp.exp(b_last - b_col + i_col - m_new) * scale)
        ct_new = decay * ct + _dot(kw.T.astype(BF16), vb)
        n_new = decay * n + jnp.sum(kw, axis=0, keepdims=True)
        return ct_new, n_new, m_new

    init = (jnp.zeros((M_QK, M_V), F32), jnp.zeros((1, M_QK), F32), jnp.zeros((1, 1), F32))
    lax.fori_loop(0, nc, step, init)


def _mlstm(z3, gates_t, b_mgate, g_mnorm):
    b, s, _ = z3.shape
    nc = s // M_CHUNK
    g4 = gates_t.reshape(GATE_ROWS, b, nc, M_CHUNK)
    qk_blk = Z_KM // M_QK
    v_blk = Z_VM // M_V
    o_blk = Z_OM // M_V
    return pl.pallas_call(
        _mlstm_kernel,
        grid=(b, M_HEADS),
        in_specs=[
            pl.BlockSpec(memory_space=pltpu.SMEM),
            pl.BlockSpec((1, s, M_QK), lambda i, h: (i, 0, h)),
            pl.BlockSpec((1, s, M_QK), lambda i, h: (i, 0, qk_blk + h)),
            pl.BlockSpec((1, s, M_V), lambda i, h: (i, 0, v_blk + h)),
            pl.BlockSpec((1, s, M_V), lambda i, h: (i, 0, o_blk + h)),
            pl.BlockSpec((1, 1, nc, M_CHUNK), lambda i, h: (h, i, 0, 0)),
            pl.BlockSpec((1, 1, nc, M_CHUNK), lambda i, h: (M_HEADS + h, i, 0, 0)),
            pl.BlockSpec((1, 1, M_V), lambda i, h: (h, 0, 0)),
        ],
        out_specs=pl.BlockSpec((1, s, M_V), lambda i, h: (i, 0, h)),
        out_shape=jax.ShapeDtypeStruct((b, s, M_V_W), BF16),
        scratch_shapes=[pltpu.VMEM((nc, M_CHUNK), F32), pltpu.VMEM((nc, M_CHUNK), F32)],
        compiler_params=_params(("parallel", "parallel")),
        name="mlstm",
    )(b_mgate, z3, z3, z3, z3, g4, g4, g_mnorm.reshape(M_HEADS, 1, M_V))


def _rope(x, cos_t, sin_t):
    return x * cos_t + pltpu.roll(x, LANES // 2, axis=1) * sin_t


def _mla_proj_kernel(ql_ref, kvl_ref, kr_ref, gq_ref, gkv_ref, wq_ref, wk_ref, wv_ref,
                     gqn_ref, gkn_ref, cos_ref, sin_ref, q_out, k_out, v_out):
    scale = QK_DIM ** -0.5
    qn = (_rms(ql_ref[...].astype(F32)) * gq_ref[...]).astype(BF16)
    kvn = (_rms(kvl_ref[...].astype(F32)) * gkv_ref[...]).astype(BF16)
    qf = _dot(qn, wq_ref[...])
    kf = _dot(kvn, wk_ref[...])
    v_out[...] = _dot(kvn, wv_ref[...]).astype(BF16)

    cos_t = cos_ref[...]
    sin_t = sin_ref[...]
    gqn = gqn_ref[...] * scale
    gkn = gkn_ref[...]
    kr = kr_ref[...].astype(F32)
    kr_ss = jnp.sum(kr * kr, axis=-1, keepdims=True)
    for h in range(MLA_HEADS):
        qh = qf[:, h * QK_PAD:(h + 1) * QK_PAD]
        qh = _rms(qh, QK_DIM) * gqn
        q_out[:, h * QK_PAD:h * QK_PAD + LANES] = qh[:, :LANES].astype(BF16)
        q_out[:, h * QK_PAD + LANES:(h + 1) * QK_PAD] = _rope(qh[:, LANES:], cos_t, sin_t).astype(BF16)

        kh = kf[:, h * NOPE_DIM:(h + 1) * NOPE_DIM]
        ss = jnp.sum(kh * kh, axis=-1, keepdims=True) + kr_ss
        r = lax.rsqrt(ss * (1.0 / QK_DIM) + NORM_EPS)
        k_out[:, h * QK_PAD:h * QK_PAD + LANES] = (kh * r * gkn[:, :LANES]).astype(BF16)
        k_out[:, h * QK_PAD + LANES:(h + 1) * QK_PAD] = _rope(kr * r * gkn[:, LANES:], cos_t, sin_t).astype(BF16)


def _mla_proj(z, g_qlat, g_kvlat, w_q, w_k, w_v, g_qn, g_kn, cos_t, sin_t, seq):
    t = z.shape[0]
    per_b = seq // TM_MLA
    full = lambda shape: pl.BlockSpec(shape, lambda i: (0,) * len(shape))
    return pl.pallas_call(
        _mla_proj_kernel,
        grid=(t // TM_MLA,),
        in_specs=[
            pl.BlockSpec((TM_MLA, Q_RANK), lambda i: (i, Z_QLAT // Q_RANK)),
            pl.BlockSpec((TM_MLA, KV_RANK), lambda i: (i, Z_KVLAT // KV_RANK)),
            pl.BlockSpec((TM_MLA, LANES), lambda i: (i, Z_KR // LANES)),
            full((1, Q_RANK)), full((1, KV_RANK)),
            full(w_q.shape), full(w_k.shape), full(w_v.shape),
            full((1, QK_PAD)), full((1, QK_PAD)),
            pl.BlockSpec((TM_MLA, LANES), lambda i: (i % per_b, 0)),
            pl.BlockSpec((TM_MLA, LANES), lambda i: (i % per_b, 0)),
        ],
        out_specs=[
            pl.BlockSpec((TM_MLA, MLA_HEADS * QK_PAD), lambda i: (i, 0)),
            pl.BlockSpec((TM_MLA, MLA_HEADS * QK_PAD), lambda i: (i, 0)),
            pl.BlockSpec((TM_MLA, MLA_W), lambda i: (i, 0)),
        ],
        out_shape=[
            jax.ShapeDtypeStruct((t, MLA_HEADS * QK_PAD), BF16),
            jax.ShapeDtypeStruct((t, MLA_HEADS * QK_PAD), BF16),
            jax.ShapeDtypeStruct((t, MLA_W), BF16),
        ],
        compiler_params=_params(("parallel",)),
        name="mla_proj",
    )(z, z, z, g_qlat.reshape(1, Q_RANK), g_kvlat.reshape(1, KV_RANK), w_q, w_k, w_v,
      g_qn, g_kn, cos_t, sin_t)


def _attn_kernel(q_ref, k_ref, v_ref, o_ref):
    s = q_ref.shape[0]
    rows = lax.broadcasted_iota(jnp.int32, (TQ, TQ), 0)
    cols = lax.broadcasted_iota(jnp.int32, (TQ, TQ), 1)
    causal = rows >= cols
    for blk in range(s // TQ):
        s0, s1 = blk * TQ, (blk + 1) * TQ
        q = q_ref[s0:s1, :]
        diag = jnp.where(causal, _dot_nt(q, k_ref[s0:s1, :]), -jnp.inf)
        m = jnp.max(diag, axis=-1, keepdims=True)
        if blk:
            past = _dot_nt(q, k_ref[:s0, :])
            m = jnp.maximum(m, jnp.max(past, axis=-1, keepdims=True))
        p = jnp.exp(diag - m)
        denom = jnp.sum(p, axis=-1, keepdims=True)
        acc = _dot(p.astype(BF16), v_ref[s0:s1, :])
        if blk:
            pp = jnp.exp(past - m)
            denom = denom + jnp.sum(pp, axis=-1, keepdims=True)
            acc = acc + _dot(pp.astype(BF16), v_ref[:s0, :])
        o_ref[s0:s1, :] = (acc * (1.0 / denom)).astype(BF16)


def _attention(q, k, v, batch, seq):
    t = q.shape[0]
    return pl.pallas_call(
        _attn_kernel,
        grid=(batch, MLA_HEADS),
        in_specs=[
            pl.BlockSpec((seq, QK_PAD), lambda b, h: (b, h)),
            pl.BlockSpec((seq, QK_PAD), lambda b, h: (b, h)),
            pl.BlockSpec((seq, MLA_V), lambda b, h: (b, h)),
        ],
        out_specs=pl.BlockSpec((seq, MLA_V), lambda b, h: (b, h)),
        out_shape=jax.ShapeDtypeStruct((t, MLA_W), BF16),
        compiler_params=_params(("parallel", "parallel")),
        name="attention",
    )(q, k, v)


def _pool_kernel(u_ref, w_ref, b_ref, s_ref, o_ref):
    s = u_ref.shape[0]
    row = lax.broadcasted_iota(jnp.int32, (s, 1), 0)
    for g, win in enumerate(POOL_WINDOWS):
        c0, c1 = g * POOL_GROUP_W, (g + 1) * POOL_GROUP_W
        u = u_ref[:, c0:c1].astype(F32)
        acc = u
        sh = 1
        while sh < win:
            acc = acc + jnp.where(row >= sh, pltpu.roll(acc, sh, axis=0), 0.0)
            sh *= 2
        inv_cnt = 1.0 / jnp.minimum(row + 1, win).astype(F32)
        pooled = (acc * inv_cnt - u).astype(BF16)
        y = _dot(pooled, w_ref[g]) + b_ref[:, c0:c1]
        o_ref[:, c0:c1] = (y * s_ref[:, c0:c1]).astype(BF16)


def _pool(z, w_pool, b_pool, s_pool, batch, seq):
    t = z.shape[0]
    return pl.pallas_call(
        _pool_kernel,
        grid=(batch,),
        in_specs=[
            pl.BlockSpec((seq, POOL_W), lambda b: (b, Z_POOL // POOL_W)),
            pl.BlockSpec(w_pool.shape, lambda b: (0, 0, 0)),
            pl.BlockSpec((1, POOL_W), lambda b: (0, 0)),
            pl.BlockSpec((1, POOL_W), lambda b: (0, 0)),
        ],
        out_specs=pl.BlockSpec((seq, POOL_W), lambda b: (b, 0)),
        out_shape=jax.ShapeDtypeStruct((t, POOL_W), BF16),
        compiler_params=_params(("parallel",)),
        name="pool",
    )(z, w_pool, b_pool.reshape(1, POOL_W), s_pool.reshape(1, POOL_W))


def _merge_kernel(a_ref, b_ref, c_ref, ga_ref, gb_ref, gc_ref, wb_ref, wo_ref, x_ref, gate_ref,
                  o_ref, acc_scr):
    j = pl.program_id(1)
    merged = (jax.nn.sigmoid(ga_ref[...].astype(F32)) * _dot(a_ref[...], wb_ref[0])
              + jax.nn.sigmoid(gb_ref[...].astype(F32)) * _dot(b_ref[...], wb_ref[1])
              + jax.nn.sigmoid(gc_ref[...].astype(F32)) * _dot(c_ref[...], wb_ref[2]))
    contrib = _dot(merged.astype(BF16), wo_ref[...])

    @pl.when(j == 0)
    def _():
        acc_scr[...] = contrib

    @pl.when(j > 0)
    def _():
        acc_scr[...] += contrib

    @pl.when(j == pl.num_programs(1) - 1)
    def _():
        o_ref[...] = x_ref[...] + gate_ref[0] * acc_scr[...]


def _merge(br_a, br_b, br_c, z, w_branch, w_out, x2, gate1, seq):
    t, d = x2.shape
    per_b = seq // TM_MERGE
    nj = d // TN_MERGE
    g0 = Z_GATES // TN_MERGE
    br = pl.BlockSpec((TM_MERGE, BRANCH_W), lambda i, j: (i, 0))
    gate = lambda k: pl.BlockSpec((TM_MERGE, TN_MERGE), lambda i, j: (i, g0 + k * nj + j))
    return pl.pallas_call(
        _merge_kernel,
        grid=(t // TM_MERGE, nj),
        in_specs=[
            br, br, br, gate(0), gate(1), gate(2),
            pl.BlockSpec((N_BRANCH, BRANCH_W, TN_MERGE), lambda i, j: (0, 0, j)),
            pl.BlockSpec((TN_MERGE, d), lambda i, j: (j, 0)),
            pl.BlockSpec((TM_MERGE, d), lambda i, j: (i, 0)),
            pl.BlockSpec((1, 1, d), lambda i, j: (i // per_b, 0, 0)),
        ],
        out_specs=pl.BlockSpec((TM_MERGE, d), lambda i, j: (i, 0)),
        out_shape=jax.ShapeDtypeStruct((t, d), F32),
        scratch_shapes=[pltpu.VMEM((TM_MERGE, d), F32)],
        compiler_params=_params(("parallel", "arbitrary")),
        name="merge",
    )(br_a, br_b, br_c, z, z, z, w_branch, w_out, x2, gate1)


def _ffn_kernel(x_ref, g_ref, sc_ref, sh_ref, gate_ref, wg_ref, wu_ref, wo_ref, o_ref,
                h_scr, acc_scr):
    j = pl.program_id(1)

    @pl.when(j == 0)
    def _():
        _norm_modulate_into(x_ref, g_ref, sc_ref, sh_ref, h_scr)

    h = h_scr[...]
    gp = _dot(h, wg_ref[...])
    up = _dot(h, wu_ref[...])
    act = ((gp * jax.nn.sigmoid(gp)) * up).astype(BF16)
    contrib = _dot(act, wo_ref[...])

    @pl.when(j == 0)
    def _():
        acc_scr[...] = contrib

    @pl.when(j > 0)
    def _():
        acc_scr[...] += contrib

    @pl.when(j == pl.num_programs(1) - 1)
    def _():
        o_ref[...] = x_ref[...] + gate_ref[0] * acc_scr[...]


def _ffn(x2, g, scale, shift, gate, w_in, w_out, seq):
    t, d = x2.shape
    per_b = seq // TM_FFN
    nj = FFN_DIM // TF_FFN
    vec = pl.BlockSpec((1, 1, d), lambda i, j: (i // per_b, 0, 0))
    return pl.pallas_call(
        _ffn_kernel,
        grid=(t // TM_FFN, nj),
        in_specs=[
            pl.BlockSpec((TM_FFN, d), lambda i, j: (i, 0)),
            pl.BlockSpec((1, d), lambda i, j: (0, 0)),
            vec, vec, vec,
            pl.BlockSpec((d, TF_FFN), lambda i, j: (0, j)),
            pl.BlockSpec((d, TF_FFN), lambda i, j: (0, nj + j)),
            pl.BlockSpec((TF_FFN, d), lambda i, j: (j, 0)),
        ],
        out_specs=pl.BlockSpec((TM_FFN, d), lambda i, j: (i, 0)),
        out_shape=jax.ShapeDtypeStruct((t, d), F32),
        scratch_shapes=[pltpu.VMEM((TM_FFN, d), BF16), pltpu.VMEM((TM_FFN, d), F32)],
        compiler_params=_params(("parallel", "arbitrary")),
        name="ffn",
    )(x2, g.reshape(1, d), scale, shift, gate, w_in, w_in, w_out)


def _rope_lanes(a, axis):
    x1, x2 = jnp.split(a, 2, axis=axis)
    zero = jnp.zeros_like(x1)
    return jnp.concatenate([x1, zero, x2, zero], axis=axis)


def _prep_w_in(w):
    d = w.shape[0]
    parts = []
    off = 0
    for size in IN_SIZES:
        parts.append(w[:, off:off + size])
        off += size
    q_m, k_m, v_m, o_m, i_m, f_m, q_lat, kv_lat, k_r, u_pool, gates = parts
    w_z = jnp.concatenate(
        [q_m, k_m, v_m, o_m, q_lat, kv_lat, _rope_lanes(k_r, 1), jnp.zeros((d, LANES), w.dtype),
         u_pool, gates], axis=1).astype(BF16)
    w_if = jnp.concatenate(
        [i_m, f_m, jnp.zeros((d, GATE_ROWS - 2 * M_HEADS), w.dtype)], axis=1).T.astype(BF16)
    return w_z, w_if


def _prep_head_cols(a):
    return jnp.concatenate([a[..., :NOPE_DIM], _rope_lanes(a[..., NOPE_DIM:], -1)], axis=-1)


def _rope_tables(seq):
    pos = jnp.arange(seq, dtype=F32)
    freqs = ROPE_THETA ** (-jnp.arange(0, ROPE_DIM, 2, dtype=F32) / ROPE_DIM)
    ang = pos[:, None] * freqs[None, :]
    cos, sin = jnp.cos(ang), jnp.sin(ang)
    zero = jnp.zeros_like(cos)
    cos_t = jnp.concatenate([cos, zero, cos, zero], axis=1)
    sin_t = jnp.concatenate([-sin, zero, sin, zero], axis=1)
    return cos_t, sin_t


def kernel(x, c, w_ada, b_ada, g_norm1, w_in, b_mgate, g_mnorm, g_qlat, w_uq, g_kvlat, w_ukv,
           g_qn, g_kn, w_pool, b_pool, s_pool, w_branch, w_out, g_norm2, w_ffn_in, w_ffn_out):
    batch, seq, d = x.shape
    depth = w_in.shape[0]
    t = batch * seq

    mod = _ada(c, w_ada, b_ada)
    cos_t, sin_t = _rope_tables(seq)
    x2 = x.reshape(t, d)

    for l in range(depth):
        shift1, scale1, gate1, shift2, scale2, gate2 = [
            mod[l, :, k * d:(k + 1) * d].reshape(batch, 1, d) for k in range(6)]

        w_z, w_if = _prep_w_in(w_in[l])
        z, gates_t = _inproj(x2, g_norm1[l], scale1, shift1, w_z, w_if, seq)

        br_a = _mlstm(z.reshape(batch, seq, Z_W), gates_t, b_mgate[l], g_mnorm[l]).reshape(t, M_V_W)

        w_q = _prep_head_cols(w_uq[l].reshape(Q_RANK, MLA_HEADS, QK_DIM)).reshape(
            Q_RANK, MLA_HEADS * QK_PAD).astype(BF16)
        w_kv = w_ukv[l].reshape(KV_RANK, MLA_HEADS, NOPE_DIM + MLA_V)
        w_k = w_kv[..., :NOPE_DIM].reshape(KV_RANK, MLA_HEADS * NOPE_DIM).astype(BF16)
        w_v = w_kv[..., NOPE_DIM:].reshape(KV_RANK, MLA_W).astype(BF16)
        q, k, v = _mla_proj(z, g_qlat[l], g_kvlat[l], w_q, w_k, w_v,
                            _prep_head_cols(g_qn[l]).reshape(1, QK_PAD),
                            _prep_head_cols(g_kn[l]).reshape(1, QK_PAD), cos_t, sin_t, seq)
        br_b = _attention(q, k, v, batch, seq)

        br_c = _pool(z, w_pool[l].astype(BF16), b_pool[l], s_pool[l], batch, seq)

        x2 = _merge(br_a, br_b, br_c, z, w_branch[l].astype(BF16), w_out[l].astype(BF16),
                    x2, gate1, seq)
        x2 = _ffn(x2, g_norm2[l], scale2, shift2, gate2, w_ffn_in[l].astype(BF16),
                  w_ffn_out[l].astype(BF16), seq)

    return x2.reshape(batch, seq, d)
```

```python
import functools

import jax
import jax.numpy as jnp
from jax import lax
from jax.experimental import pallas as pl
from jax.experimental.pallas import tpu as pltpu

F32 = jnp.float32
BF16 = jnp.bfloat16

D_MODEL = 2048
M_HEADS = 4
M_QK = 128
M_V = 256
M_QK_W = M_HEADS * M_QK
M_V_W = M_HEADS * M_V
M_CHUNK = 128
MLA_HEADS = 8
NOPE_DIM = 128
ROPE_DIM = 64
ROPE_HALF = ROPE_DIM // 2
QK_DIM = NOPE_DIM + ROPE_DIM
MLA_V = 128
MLA_W = MLA_HEADS * MLA_V
Q_RANK = 512
KV_RANK = 256
ROPE_THETA = 10000.0
POOL_WINDOWS = (2, 4, 8, 16)
POOL_GROUPS = 4
POOL_GROUP_W = 256
POOL_W = POOL_GROUPS * POOL_GROUP_W
N_BRANCH = 3
BRANCH_W = 1024
FFN_DIM = ((8 * D_MODEL // 3 + 255) // 256) * 256
NORM_EPS = 1e-6
IN_SIZES = (M_QK_W, M_QK_W, M_V_W, M_V_W, M_HEADS, M_HEADS, Q_RANK, KV_RANK, ROPE_DIM, POOL_W,
            N_BRANCH * D_MODEL)

LANES = 128
QK_PAD = 2 * LANES
GATE_ROWS = 16

Z_QM = 0
Z_QLAT = Z_QM + M_QK_W
Z_VM = Z_QLAT + Q_RANK
Z_OM = Z_VM + M_V_W
Z_KVLAT = Z_OM + M_V_W
Z_KR = Z_KVLAT + KV_RANK
Z_POOL = Z_KR + 2 * LANES
Z_GATES = Z_POOL + POOL_W
Z_W = Z_GATES + N_BRANCH * D_MODEL

V7X_VMEM_LIMIT = 52 * 1024 * 1024

TM_IN = 1024
TN_IN = 1536
NORM_ROWS = 64
NORM_UNROLL = 4
TM_MLA = 512
TQ = 256
TM_MERGE = 512
TN_MERGE = 512
TM_FFN = 512
TF_FFN = 512
TN_ADA = 1024


def _params(semantics):
    return pltpu.CompilerParams(dimension_semantics=semantics, vmem_limit_bytes=V7X_VMEM_LIMIT)


def _rms(x, width=None):
    ss = jnp.sum(x * x, axis=-1, keepdims=True)
    n = x.shape[-1] if width is None else width
    return x * lax.rsqrt(ss * (1.0 / n) + NORM_EPS)


def _dot(a, b):
    return jnp.dot(a, b, preferred_element_type=F32)


def _dot_nt(a, b):
    return lax.dot_general(a, b, (((1,), (1,)), ((), ())), preferred_element_type=F32)


def _ada_kernel(c_ref, w_ref, b_ref, o_ref):
    c = c_ref[...]
    ca = (c * jax.nn.sigmoid(c)).astype(BF16)
    o_ref[0] = _dot(ca, w_ref[0].astype(BF16)) + b_ref[0]


def _ada(c, w_ada, b_ada):
    depth, d, n = w_ada.shape
    b = c.shape[0]
    return pl.pallas_call(
        _ada_kernel,
        grid=(depth, n // TN_ADA),
        in_specs=[
            pl.BlockSpec((b, d), lambda l, j: (0, 0)),
            pl.BlockSpec((1, d, TN_ADA), lambda l, j: (l, 0, j)),
            pl.BlockSpec((1, 1, TN_ADA), lambda l, j: (l, 0, j)),
        ],
        out_specs=pl.BlockSpec((1, b, TN_ADA), lambda l, j: (l, 0, j)),
        out_shape=jax.ShapeDtypeStruct((depth, b, n), F32),
        compiler_params=_params(("parallel", "parallel")),
        name="adaln",
    )(c, w_ada, b_ada.reshape(depth, 1, n))


def _norm_modulate_into(x_ref, g_ref, sc_ref, sh_ref, h_scr, r_scr):
    n_slabs = x_ref.shape[0] // NORM_ROWS
    d = x_ref.shape[1]

    def stats(r, carry):
        r0 = pl.multiple_of(r * NORM_ROWS, NORM_ROWS)
        x = x_ref[pl.ds(r0, NORM_ROWS), :]
        ms = jnp.sum(x * x, axis=-1, keepdims=True) * (1.0 / d)
        r_scr[pl.ds(r0, NORM_ROWS), :] = jnp.broadcast_to(lax.rsqrt(ms + NORM_EPS), (NORM_ROWS, LANES))
        return carry

    lax.fori_loop(0, n_slabs, stats, 0, unroll=NORM_UNROLL)

    gain = g_ref[...] * (1.0 + sc_ref[0])
    shift = sh_ref[0]

    def apply(r, carry):
        r0 = pl.multiple_of(r * NORM_ROWS, NORM_ROWS)
        rstd = r_scr[pl.ds(r0, NORM_ROWS), :]
        for t in range(d // LANES):
            c0, c1 = t * LANES, (t + 1) * LANES
            x = x_ref[pl.ds(r0, NORM_ROWS), c0:c1]
            h_scr[pl.ds(r0, NORM_ROWS), c0:c1] = (x * rstd * gain[:, c0:c1] + shift[:, c0:c1]).astype(BF16)
        return carry

    lax.fori_loop(0, n_slabs, apply, 0)


def _inproj_kernel(x_ref, g_ref, sc_ref, sh_ref, w_ref, wkt_ref, wif_ref, z_ref, kt_ref, gt_ref,
                   h_scr, r_scr):
    @pl.when(pl.program_id(1) == 0)
    def _():
        _norm_modulate_into(x_ref, g_ref, sc_ref, sh_ref, h_scr, r_scr)
        h = h_scr[...]
        kt_ref[...] = _dot_nt(wkt_ref[...], h).astype(BF16)
        gt_ref[...] = _dot_nt(wif_ref[...], h)

    z_ref[...] = _dot(h_scr[...], w_ref[...]).astype(BF16)


def _inproj(x2, g, scale, shift, w_z, w_kt, w_if, seq):
    t, d = x2.shape
    per_b = seq // TM_IN
    vec = pl.BlockSpec((1, 1, d), lambda i, j: (i // per_b, 0, 0))
    return pl.pallas_call(
        _inproj_kernel,
        grid=(t // TM_IN, Z_W // TN_IN),
        in_specs=[
            pl.BlockSpec((TM_IN, d), lambda i, j: (i, 0)),
            pl.BlockSpec((1, d), lambda i, j: (0, 0)),
            vec, vec,
            pl.BlockSpec((d, TN_IN), lambda i, j: (0, j)),
            pl.BlockSpec((M_QK_W, d), lambda i, j: (0, 0)),
            pl.BlockSpec((GATE_ROWS, d), lambda i, j: (0, 0)),
        ],
        out_specs=[
            pl.BlockSpec((TM_IN, TN_IN), lambda i, j: (i, j)),
            pl.BlockSpec((M_QK_W, TM_IN), lambda i, j: (0, i)),
            pl.BlockSpec((GATE_ROWS, TM_IN), lambda i, j: (0, i)),
        ],
        out_shape=[
            jax.ShapeDtypeStruct((t, Z_W), BF16),
            jax.ShapeDtypeStruct((M_QK_W, t), BF16),
            jax.ShapeDtypeStruct((GATE_ROWS, t), F32),
        ],
        scratch_shapes=[pltpu.VMEM((TM_IN, d), BF16), pltpu.VMEM((TM_IN, LANES), F32)],
        compiler_params=_params(("parallel", "arbitrary")),
        name="inproj",
    )(x2, g.reshape(1, d), scale, shift, w_z, w_kt, w_if)


def _mlstm_kernel(bias_ref, q_ref, kt_ref, v_ref, o_ref, g_ref, gn_ref, out_ref,
                  b_scr, i_scr, ct_scr, nr_scr):
    L = M_CHUNK
    nc = g_ref.shape[2]
    scale = M_QK ** -0.5

    lane = lax.broadcasted_iota(jnp.int32, (nc, L), 1)
    for h in range(M_HEADS):
        fg = g_ref[M_HEADS + h, 0] + bias_ref[1, h]
        bc = jnp.minimum(fg, 0.0) - jnp.log(1.0 + jnp.exp(-jnp.abs(fg)))
        sh = 1
        while sh < L:
            bc = bc + jnp.where(lane >= sh, pltpu.roll(bc, sh, axis=1), 0.0)
            sh *= 2
        b_scr[h] = bc
        i_scr[h] = g_ref[h, 0] + bias_ref[0, h]
    ct_scr[...] = jnp.zeros_like(ct_scr)
    nr_scr[...] = jnp.zeros_like(nr_scr)

    rows = lax.broadcasted_iota(jnp.int32, (L, L), 0)
    cols = lax.broadcasted_iota(jnp.int32, (L, L), 1)
    eye = rows == cols
    tril = rows >= cols
    ones = jnp.ones((L, LANES), BF16)

    def to_col(row):
        return jnp.sum(jnp.where(eye, jnp.broadcast_to(row, (L, L)), 0.0), axis=1, keepdims=True)

    def step(c, ms):
        r0 = pl.multiple_of(c * L, L)
        heads = range(M_HEADS)
        qb = [q_ref[0, pl.ds(r0, L), h * M_QK:(h + 1) * M_QK] for h in heads]
        kt = [kt_ref[h * M_QK:(h + 1) * M_QK, pl.ds(r0, L)] for h in heads]
        vb = [v_ref[0, pl.ds(r0, L), h * M_V:(h + 1) * M_V] for h in heads]
        b_row = [b_scr[h, pl.ds(c, 1), :] for h in heads]
        i_row = [i_scr[h, pl.ds(c, 1), :] for h in heads]
        ct = [ct_scr[h] for h in heads]
        nr = [nr_scr[h] for h in heads]

        qk = [_dot(qb[h], kt[h]) for h in heads]
        inter = [_dot(qb[h], ct[h].astype(BF16)) for h in heads]
        qn = [_dot(qb[h], nr[h].astype(BF16)) for h in heads]

        m_out = []
        for h in heads:
            b_last = b_row[h][:, L - 1:L]
            g_row = b_last - b_row[h] + i_row[h]
            m_new = jnp.maximum(b_last + ms[h], jnp.max(g_row, axis=1, keepdims=True))
            decay = jnp.exp(b_last + ms[h] - m_new)
            ktw = (kt[h].astype(F32) * (jnp.exp(g_row - m_new) * scale)).astype(BF16)
            ct_scr[h] = decay * ct[h] + _dot(ktw, vb[h])
            nr_scr[h] = decay * nr[h] + _dot(ktw, ones)
            m_out.append(m_new)

        w_inter, e_neg, sb = [], [], []
        for h in heads:
            b_col = to_col(b_row[h])
            a_col = b_col + ms[h]
            dm = jnp.where(tril, b_col - b_row[h] + i_row[h], -jnp.inf)
            m_t = jnp.maximum(a_col, jnp.max(dm, axis=1, keepdims=True))
            w_inter.append(jnp.exp(a_col - m_t))
            e_neg.append(jnp.exp(-m_t))
            sb.append(((qk[h] * scale) * jnp.exp(dm - m_t)).astype(BF16))

        sv = [_dot(sb[h], vb[h]) for h in heads]
        ssum = [_dot(sb[h], ones) for h in heads]

        for h in heads:
            num = w_inter[h] * inter[h] + sv[h]
            den = w_inter[h] * qn[h] + ssum[h]
            inv = 1.0 / jnp.maximum(jnp.abs(den), e_neg[h])
            hh = num * jnp.concatenate([inv] * (M_V // LANES), axis=1)
            o = o_ref[0, pl.ds(r0, L), h * M_V:(h + 1) * M_V].astype(F32)
            out_ref[0, pl.ds(r0, L), h * M_V:(h + 1) * M_V] = (
                (_rms(hh) * gn_ref[h:h + 1, :]) * jax.nn.sigmoid(o)).astype(BF16)
        return tuple(m_out)

    lax.fori_loop(0, nc, step, tuple(jnp.zeros((1, 1), F32) for _ in range(M_HEADS)))


def _mlstm(z3, kt, gates_t, b_mgate, g_mnorm):
    b, s, _ = z3.shape
    nc = s // M_CHUNK
    g4 = gates_t.reshape(GATE_ROWS, b, nc, M_CHUNK)
    return pl.pallas_call(
        _mlstm_kernel,
        grid=(b,),
        in_specs=[
            pl.BlockSpec(memory_space=pltpu.SMEM),
            pl.BlockSpec((1, s, M_QK_W), lambda i: (i, 0, Z_QM // M_QK_W)),
            pl.BlockSpec((M_QK_W, s), lambda i: (0, i)),
            pl.BlockSpec((1, s, M_V_W), lambda i: (i, 0, Z_VM // M_V_W)),
            pl.BlockSpec((1, s, M_V_W), lambda i: (i, 0, Z_OM // M_V_W)),
            pl.BlockSpec((GATE_ROWS, 1, nc, M_CHUNK), lambda i: (0, i, 0, 0)),
            pl.BlockSpec((M_HEADS, M_V), lambda i: (0, 0)),
        ],
        out_specs=pl.BlockSpec((1, s, M_V_W), lambda i: (i, 0, 0)),
        out_shape=jax.ShapeDtypeStruct((b, s, M_V_W), BF16),
        scratch_shapes=[
            pltpu.VMEM((M_HEADS, nc, M_CHUNK), F32), pltpu.VMEM((M_HEADS, nc, M_CHUNK), F32),
            pltpu.VMEM((M_HEADS, M_QK, M_V), F32), pltpu.VMEM((M_HEADS, M_QK, LANES), F32),
        ],
        compiler_params=_params(("parallel",)),
        name="mlstm",
    )(b_mgate, z3, kt, z3, z3, g4, g_mnorm)


def _rope(x, cos_t, sin_t):
    return x * cos_t + pltpu.roll(x, LANES // 2, axis=1) * sin_t


def _mla_proj_kernel(ql_ref, kvl_ref, kr_ref, gq_ref, gkv_ref, wq_ref, wk_ref, wv_ref,
                     gqn_ref, gkn_ref, cos_ref, sin_ref, q_out, k_out, v_out):
    scale = QK_DIM ** -0.5
    qn = (_rms(ql_ref[...].astype(F32)) * gq_ref[...]).astype(BF16)
    kvn = (_rms(kvl_ref[...].astype(F32)) * gkv_ref[...]).astype(BF16)
    qf = _dot(qn, wq_ref[...])
    kf = _dot(kvn, wk_ref[...])
    v_out[...] = _dot(kvn, wv_ref[...]).astype(BF16)

    cos_t = cos_ref[...]
    sin_t = sin_ref[...]
    gqn = gqn_ref[...] * scale
    gkn = gkn_ref[...]
    kr = kr_ref[...].astype(F32)
    kr_ss = jnp.sum(kr * kr, axis=-1, keepdims=True)
    for h in range(MLA_HEADS):
        qh = qf[:, h * QK_PAD:(h + 1) * QK_PAD]
        qh = _rms(qh, QK_DIM) * gqn
        q_out[:, h * QK_PAD:h * QK_PAD + LANES] = qh[:, :LANES].astype(BF16)
        q_out[:, h * QK_PAD + LANES:(h + 1) * QK_PAD] = _rope(qh[:, LANES:], cos_t, sin_t).astype(BF16)

        kh = kf[:, h * NOPE_DIM:(h + 1) * NOPE_DIM]
        ss = jnp.sum(kh * kh, axis=-1, keepdims=True) + kr_ss
        r = lax.rsqrt(ss * (1.0 / QK_DIM) + NORM_EPS)
        k_out[:, h * QK_PAD:h * QK_PAD + LANES] = (kh * r * gkn[:, :LANES]).astype(BF16)
        k_out[:, h * QK_PAD + LANES:(h + 1) * QK_PAD] = _rope(kr * r * gkn[:, LANES:], cos_t, sin_t).astype(BF16)


def _mla_proj(z, g_qlat, g_kvlat, w_q, w_k, w_v, g_qn, g_kn, cos_t, sin_t, seq):
    t = z.shape[0]
    per_b = seq // TM_MLA
    full = lambda shape: pl.BlockSpec(shape, lambda i: (0,) * len(shape))
    return pl.pallas_call(
        _mla_proj_kernel,
        grid=(t // TM_MLA,),
        in_specs=[
            pl.BlockSpec((TM_MLA, Q_RANK), lambda i: (i, Z_QLAT // Q_RANK)),
            pl.BlockSpec((TM_MLA, KV_RANK), lambda i: (i, Z_KVLAT // KV_RANK)),
            pl.BlockSpec((TM_MLA, LANES), lambda i: (i, Z_KR // LANES)),
            full((1, Q_RANK)), full((1, KV_RANK)),
            full(w_q.shape), full(w_k.shape), full(w_v.shape),
            full((1, QK_PAD)), full((1, QK_PAD)),
            pl.BlockSpec((TM_MLA, LANES), lambda i: (i % per_b, 0)),
            pl.BlockSpec((TM_MLA, LANES), lambda i: (i % per_b, 0)),
        ],
        out_specs=[
            pl.BlockSpec((TM_MLA, MLA_HEADS * QK_PAD), lambda i: (i, 0)),
            pl.BlockSpec((TM_MLA, MLA_HEADS * QK_PAD), lambda i: (i, 0)),
            pl.BlockSpec((TM_MLA, MLA_W), lambda i: (i, 0)),
        ],
        out_shape=[
            jax.ShapeDtypeStruct((t, MLA_HEADS * QK_PAD), BF16),
            jax.ShapeDtypeStruct((t, MLA_HEADS * QK_PAD), BF16),
            jax.ShapeDtypeStruct((t, MLA_W), BF16),
        ],
        compiler_params=_params(("parallel",)),
        name="mla_proj",
    )(z, z, z, g_qlat.reshape(1, Q_RANK), g_kvlat.reshape(1, KV_RANK), w_q, w_k, w_v,
      g_qn, g_kn, cos_t, sin_t)


def _attn_kernel(q_ref, k_ref, v_ref, o_ref):
    s = q_ref.shape[0]
    rows = lax.broadcasted_iota(jnp.int32, (TQ, TQ), 0)
    cols = lax.broadcasted_iota(jnp.int32, (TQ, TQ), 1)
    causal = rows >= cols
    for blk in range(s // TQ):
        s0, s1 = blk * TQ, (blk + 1) * TQ
        q = q_ref[s0:s1, :]
        diag = jnp.where(causal, _dot_nt(q, k_ref[s0:s1, :]), -jnp.inf)
        m = jnp.max(diag, axis=-1, keepdims=True)
        if blk:
            past = _dot_nt(q, k_ref[:s0, :])
            m = jnp.maximum(m, jnp.max(past, axis=-1, keepdims=True))
        p = jnp.exp(diag - m)
        denom = jnp.sum(p, axis=-1, keepdims=True)
        acc = _dot(p.astype(BF16), v_ref[s0:s1, :])
        if blk:
            pp = jnp.exp(past - m)
            denom = denom + jnp.sum(pp, axis=-1, keepdims=True)
            acc = acc + _dot(pp.astype(BF16), v_ref[:s0, :])
        o_ref[s0:s1, :] = (acc * (1.0 / denom)).astype(BF16)


def _attention(q, k, v, batch, seq):
    t = q.shape[0]
    return pl.pallas_call(
        _attn_kernel,
        grid=(batch, MLA_HEADS),
        in_specs=[
            pl.BlockSpec((seq, QK_PAD), lambda b, h: (b, h)),
            pl.BlockSpec((seq, QK_PAD), lambda b, h: (b, h)),
            pl.BlockSpec((seq, MLA_V), lambda b, h: (b, h)),
        ],
        out_specs=pl.BlockSpec((seq, MLA_V), lambda b, h: (b, h)),
        out_shape=jax.ShapeDtypeStruct((t, MLA_W), BF16),
        compiler_params=_params(("parallel", "parallel")),
        name="attention",
    )(q, k, v)


def _pool_kernel(u0_ref, u1_ref, u2_ref, u3_ref, w_ref, b_ref, s_ref, o_ref):
    s = o_ref.shape[0]
    row = lax.broadcasted_iota(jnp.int32, (s, 1), 0)
    for g, (u_ref, win) in enumerate(zip((u0_ref, u1_ref, u2_ref, u3_ref), POOL_WINDOWS)):
        c0, c1 = g * POOL_GROUP_W, (g + 1) * POOL_GROUP_W
        u = u_ref[...].astype(F32)
        acc = u
        sh = 1
        while sh < win:
            acc = acc + jnp.where(row >= sh, pltpu.roll(acc, sh, axis=0), 0.0)
            sh *= 2
        inv_cnt = 1.0 / jnp.minimum(row + 1, win).astype(F32)
        pooled = (acc * inv_cnt - u).astype(BF16)
        y = _dot(pooled, w_ref[g]) + b_ref[:, c0:c1]
        o_ref[:, c0:c1] = (y * s_ref[:, c0:c1]).astype(BF16)


def _pool(z, w_pool, layer, b_pool, s_pool, batch, seq):
    t = z.shape[0]
    group = lambda g: pl.BlockSpec((seq, POOL_GROUP_W), lambda b: (b, Z_POOL // POOL_GROUP_W + g))
    return pl.pallas_call(
        _pool_kernel,
        grid=(batch,),
        in_specs=[
            group(0), group(1), group(2), group(3),
            pl.BlockSpec((None,) + w_pool.shape[1:], lambda b: (layer, 0, 0, 0)),
            pl.BlockSpec((1, POOL_W), lambda b: (0, 0)),
            pl.BlockSpec((1, POOL_W), lambda b: (0, 0)),
        ],
        out_specs=pl.BlockSpec((seq, POOL_W), lambda b: (b, 0)),
        out_shape=jax.ShapeDtypeStruct((t, POOL_W), BF16),
        compiler_params=_params(("parallel",)),
        name="pool",
    )(z, z, z, z, w_pool, b_pool.reshape(1, POOL_W), s_pool.reshape(1, POOL_W))


def _merge_kernel(a_ref, b_ref, c_ref, ga_ref, gb_ref, gc_ref, wb_ref, wo_ref, x_ref, gate_ref,
                  o_ref, acc_scr):
    j = pl.program_id(1)

    @pl.when(j == 0)
    def _():
        acc_scr[...] = jnp.zeros_like(acc_scr)

    merged = (jax.nn.sigmoid(ga_ref[...].astype(F32)) * _dot(a_ref[...], wb_ref[0])
              + jax.nn.sigmoid(gb_ref[...].astype(F32)) * _dot(b_ref[...], wb_ref[1])
              + jax.nn.sigmoid(gc_ref[...].astype(F32)) * _dot(c_ref[...], wb_ref[2]))
    acc_scr[...] += _dot(merged.astype(BF16), wo_ref[...])

    @pl.when(j == pl.num_programs(1) - 1)
    def _():
        o_ref[...] = x_ref[...] + gate_ref[0] * acc_scr[...]


def _merge(br_a, br_b, br_c, z, w_branch, w_out, layer, x2, gate1, seq):
    t, d = x2.shape
    per_b = seq // TM_MERGE
    nj = d // TN_MERGE
    g0 = Z_GATES // TN_MERGE
    br = pl.BlockSpec((TM_MERGE, BRANCH_W), lambda i, j: (i, 0))
    gate = lambda k: pl.BlockSpec((TM_MERGE, TN_MERGE), lambda i, j: (i, g0 + k * nj + j))
    return pl.pallas_call(
        _merge_kernel,
        grid=(t // TM_MERGE, nj),
        in_specs=[
            br, br, br, gate(0), gate(1), gate(2),
            pl.BlockSpec((None, N_BRANCH, BRANCH_W, TN_MERGE), lambda i, j: (layer, 0, 0, j)),
            pl.BlockSpec((None, TN_MERGE, d), lambda i, j: (layer, j, 0)),
            pl.BlockSpec((TM_MERGE, d), lambda i, j: (i, 0)),
            pl.BlockSpec((1, 1, d), lambda i, j: (i // per_b, 0, 0)),
        ],
        out_specs=pl.BlockSpec((TM_MERGE, d), lambda i, j: (i, 0)),
        out_shape=jax.ShapeDtypeStruct((t, d), F32),
        scratch_shapes=[pltpu.VMEM((TM_MERGE, d), F32)],
        compiler_params=_params(("parallel", "arbitrary")),
        name="merge",
    )(br_a, br_b, br_c, z, z, z, w_branch, w_out, x2, gate1)


def _ffn_kernel(x_ref, g_ref, sc_ref, sh_ref, gate_ref, wg_ref, wu_ref, wo_ref, o_ref,
                h_scr, r_scr, acc_scr):
    j = pl.program_id(1)

    @pl.when(j == 0)
    def _():
        _norm_modulate_into(x_ref, g_ref, sc_ref, sh_ref, h_scr, r_scr)
        acc_scr[...] = jnp.zeros_like(acc_scr)

    h = h_scr[...]
    gp = _dot(h, wg_ref[...])
    up = _dot(h, wu_ref[...])
    act = ((gp * jax.nn.sigmoid(gp)) * up).astype(BF16)
    acc_scr[...] += _dot(act, wo_ref[...])

    @pl.when(j == pl.num_programs(1) - 1)
    def _():
        o_ref[...] = x_ref[...] + gate_ref[0] * acc_scr[...]


def _ffn(x2, g, scale, shift, gate, w_in, w_out, layer, seq):
    t, d = x2.shape
    per_b = seq // TM_FFN
    nj = FFN_DIM // TF_FFN
    vec = pl.BlockSpec((1, 1, d), lambda i, j: (i // per_b, 0, 0))
    return pl.pallas_call(
        _ffn_kernel,
        grid=(t // TM_FFN, nj),
        in_specs=[
            pl.BlockSpec((TM_FFN, d), lambda i, j: (i, 0)),
            pl.BlockSpec((1, d), lambda i, j: (0, 0)),
            vec, vec, vec,
            pl.BlockSpec((None, d, TF_FFN), lambda i, j: (layer, 0, j)),
            pl.BlockSpec((None, d, TF_FFN), lambda i, j: (layer, 0, nj + j)),
            pl.BlockSpec((None, TF_FFN, d), lambda i, j: (layer, j, 0)),
        ],
        out_specs=pl.BlockSpec((TM_FFN, d), lambda i, j: (i, 0)),
        out_shape=jax.ShapeDtypeStruct((t, d), F32),
        scratch_shapes=[pltpu.VMEM((TM_FFN, d), BF16), pltpu.VMEM((TM_FFN, LANES), F32),
                        pltpu.VMEM((TM_FFN, d), F32)],
        compiler_params=_params(("parallel", "arbitrary")),
        name="ffn",
    )(x2, g.reshape(1, d), scale, shift, gate, w_in, w_in, w_out)


def _rope_lanes(a, axis):
    x1, x2 = jnp.split(a, 2, axis=axis)
    zero = jnp.zeros_like(x1)
    return jnp.concatenate([x1, zero, x2, zero], axis=axis)


def _prep_w_in(w):
    d = w.shape[0]
    parts = []
    off = 0
    for size in IN_SIZES:
        parts.append(w[:, off:off + size])
        off += size
    q_m, k_m, v_m, o_m, i_m, f_m, q_lat, kv_lat, k_r, u_pool, gates = parts
    w_z = jnp.concatenate(
        [q_m, q_lat, v_m, o_m, kv_lat, _rope_lanes(k_r, 1), jnp.zeros((d, LANES), w.dtype),
         u_pool, gates], axis=1).astype(BF16)
    w_kt = k_m.T.astype(BF16)
    w_if = jnp.concatenate(
        [i_m, f_m, jnp.zeros((d, GATE_ROWS - 2 * M_HEADS), w.dtype)], axis=1).T.astype(BF16)
    return w_z, w_kt, w_if


def _prep_head_cols(a):
    return jnp.concatenate([a[..., :NOPE_DIM], _rope_lanes(a[..., NOPE_DIM:], -1)], axis=-1)


def _rope_tables(seq):
    pos = jnp.arange(seq, dtype=F32)
    freqs = ROPE_THETA ** (-jnp.arange(0, ROPE_DIM, 2, dtype=F32) / ROPE_DIM)
    ang = pos[:, None] * freqs[None, :]
    cos, sin = jnp.cos(ang), jnp.sin(ang)
    zero = jnp.zeros_like(cos)
    cos_t = jnp.concatenate([cos, zero, cos, zero], axis=1)
    sin_t = jnp.concatenate([-sin, zero, sin, zero], axis=1)
    return cos_t, sin_t


def kernel(x, c, w_ada, b_ada, g_norm1, w_in, b_mgate, g_mnorm, g_qlat, w_uq, g_kvlat, w_ukv,
           g_qn, g_kn, w_pool, b_pool, s_pool, w_branch, w_out, g_norm2, w_ffn_in, w_ffn_out):
    batch, seq, d = x.shape
    depth = w_in.shape[0]
    t = batch * seq

    mod = _ada(c, w_ada, b_ada)
    cos_t, sin_t = _rope_tables(seq)
    x2 = x.reshape(t, d)
    w_pool_b, w_branch_b, w_out_b = w_pool.astype(BF16), w_branch.astype(BF16), w_out.astype(BF16)
    w_ffn_in_b, w_ffn_out_b = w_ffn_in.astype(BF16), w_ffn_out.astype(BF16)

    for l in range(depth):
        shift1, scale1, gate1, shift2, scale2, gate2 = [
            mod[l, :, k * d:(k + 1) * d].reshape(batch, 1, d) for k in range(6)]

        w_z, w_kt, w_if = _prep_w_in(w_in[l])
        z, kt, gates_t = _inproj(x2, g_norm1[l], scale1, shift1, w_z, w_kt, w_if, seq)

        br_a = _mlstm(z.reshape(batch, seq, Z_W), kt, gates_t, b_mgate[l],
                      g_mnorm[l]).reshape(t, M_V_W)

        w_q = _prep_head_cols(w_uq[l].reshape(Q_RANK, MLA_HEADS, QK_DIM)).reshape(
            Q_RANK, MLA_HEADS * QK_PAD).astype(BF16)
        w_kv = w_ukv[l].reshape(KV_RANK, MLA_HEADS, NOPE_DIM + MLA_V)
        w_k = w_kv[..., :NOPE_DIM].reshape(KV_RANK, MLA_HEADS * NOPE_DIM).astype(BF16)
        w_v = w_kv[..., NOPE_DIM:].reshape(KV_RANK, MLA_W).astype(BF16)
        q, k, v = _mla_proj(z, g_qlat[l], g_kvlat[l], w_q, w_k, w_v,
                            _prep_head_cols(g_qn[l]).reshape(1, QK_PAD),
                            _prep_head_cols(g_kn[l]).reshape(1, QK_PAD), cos_t, sin_t, seq)
        br_b = _attention(q, k, v, batch, seq)

        br_c = _pool(z, w_pool_b, l, b_pool[l], s_pool[l], batch, seq)

        x2 = _merge(br_a, br_b, br_c, z, w_branch_b, w_out_b, l, x2, gate1, seq)
        x2 = _ffn(x2, g_norm2[l], scale2, shift2, gate2, w_ffn_in_b, w_ffn_out_b, l, seq)

    return x2.reshape(batch, seq, d)
```

```python
import functools

import jax
import jax.numpy as jnp
from jax import lax
from jax.experimental import pallas as pl
from jax.experimental.pallas import tpu as pltpu

F32 = jnp.float32
BF16 = jnp.bfloat16

D_MODEL = 2048
M_HEADS = 4
M_QK = 128
M_V = 256
M_QK_W = M_HEADS * M_QK
M_V_W = M_HEADS * M_V
M_CHUNK = 128
MLA_HEADS = 8
NOPE_DIM = 128
ROPE_DIM = 64
ROPE_HALF = ROPE_DIM // 2
QK_DIM = NOPE_DIM + ROPE_DIM
MLA_V = 128
MLA_W = MLA_HEADS * MLA_V
Q_RANK = 512
KV_RANK = 256
ROPE_THETA = 10000.0
POOL_WINDOWS = (2, 4, 8, 16)
POOL_GROUPS = 4
POOL_GROUP_W = 256
POOL_W = POOL_GROUPS * POOL_GROUP_W
N_BRANCH = 3
BRANCH_W = 1024
FFN_DIM = ((8 * D_MODEL // 3 + 255) // 256) * 256
NORM_EPS = 1e-6
IN_SIZES = (M_QK_W, M_QK_W, M_V_W, M_V_W, M_HEADS, M_HEADS, Q_RANK, KV_RANK, ROPE_DIM, POOL_W,
            N_BRANCH * D_MODEL)

LANES = 128
QK_PAD = 2 * LANES
GATE_ROWS = 16

Z_GATES = 0
Z_QM = Z_GATES + N_BRANCH * D_MODEL
Z_QLAT = Z_QM + M_QK_W
Z_VM = Z_QLAT + Q_RANK
Z_OM = Z_VM + M_V_W
Z_KVLAT = Z_OM + M_V_W
Z_KR = Z_KVLAT + KV_RANK
Z_POOL = Z_KR + 2 * LANES
Z_W = Z_POOL + POOL_W

V7X_VMEM_LIMIT = 52 * 1024 * 1024

TM_IN = 1024
TN_IN = 1536
NORM_ROWS = 64
NORM_UNROLL = 4
TM_MLA = 512
TQ = 256
TM_MERGE = 256
TM_FFN = 1024
TM_FFN_SUB = 512
TF_FFN = 256
TN_ADA = 1024


def _params(semantics):
    return pltpu.CompilerParams(dimension_semantics=semantics, vmem_limit_bytes=V7X_VMEM_LIMIT)


def _rms(x, width=None):
    ss = jnp.sum(x * x, axis=-1, keepdims=True)
    n = x.shape[-1] if width is None else width
    return x * lax.rsqrt(ss * (1.0 / n) + NORM_EPS)


def _dot(a, b):
    return jnp.dot(a, b, preferred_element_type=F32)


def _dot_nt(a, b):
    return lax.dot_general(a, b, (((1,), (1,)), ((), ())), preferred_element_type=F32)


def _ada_kernel(c_ref, w_ref, b_ref, o_ref):
    c = c_ref[...]
    ca = (c * jax.nn.sigmoid(c)).astype(BF16)
    o_ref[0] = _dot(ca, w_ref[0].astype(BF16)) + b_ref[0]


def _ada(c, w_ada, b_ada):
    depth, d, n = w_ada.shape
    b = c.shape[0]
    return pl.pallas_call(
        _ada_kernel,
        grid=(depth, n // TN_ADA),
        in_specs=[
            pl.BlockSpec((b, d), lambda l, j: (0, 0)),
            pl.BlockSpec((1, d, TN_ADA), lambda l, j: (l, 0, j)),
            pl.BlockSpec((1, 1, TN_ADA), lambda l, j: (l, 0, j)),
        ],
        out_specs=pl.BlockSpec((1, b, TN_ADA), lambda l, j: (l, 0, j)),
        out_shape=jax.ShapeDtypeStruct((depth, b, n), F32),
        compiler_params=_params(("parallel", "parallel")),
        name="adaln",
    )(c, w_ada, b_ada.reshape(depth, 1, n))


def _norm_modulate_into(x_ref, g_ref, sc_ref, sh_ref, h_scr, r_scr):
    n_slabs = x_ref.shape[0] // NORM_ROWS
    d = x_ref.shape[1]

    def stats(r, carry):
        r0 = pl.multiple_of(r * NORM_ROWS, NORM_ROWS)
        x = x_ref[pl.ds(r0, NORM_ROWS), :]
        ms = jnp.sum(x * x, axis=-1, keepdims=True) * (1.0 / d)
        r_scr[pl.ds(r0, NORM_ROWS), :] = jnp.broadcast_to(lax.rsqrt(ms + NORM_EPS), (NORM_ROWS, LANES))
        return carry

    lax.fori_loop(0, n_slabs, stats, 0, unroll=NORM_UNROLL)

    gain = g_ref[...] * (1.0 + sc_ref[0])
    shift = sh_ref[0]

    def apply(r, carry):
        r0 = pl.multiple_of(r * NORM_ROWS, NORM_ROWS)
        rstd = r_scr[pl.ds(r0, NORM_ROWS), :]
        for t in range(d // LANES):
            c0, c1 = t * LANES, (t + 1) * LANES
            x = x_ref[pl.ds(r0, NORM_ROWS), c0:c1]
            h_scr[pl.ds(r0, NORM_ROWS), c0:c1] = (x * rstd * gain[:, c0:c1] + shift[:, c0:c1]).astype(BF16)
        return carry

    lax.fori_loop(0, n_slabs, apply, 0)


def _inproj_kernel(x_ref, g_ref, sc_ref, sh_ref, w_ref, wkt_ref, wif_ref, z_ref, kt_ref, gt_ref,
                   h_scr, r_scr):
    @pl.when(pl.program_id(1) == 0)
    def _():
        _norm_modulate_into(x_ref, g_ref, sc_ref, sh_ref, h_scr, r_scr)
        h = h_scr[...]
        kt_ref[...] = _dot_nt(wkt_ref[...], h).astype(BF16)
        gt_ref[...] = _dot_nt(wif_ref[...], h)

    z_ref[...] = _dot(h_scr[...], w_ref[...]).astype(BF16)


def _inproj(x2, g, scale, shift, w_z, w_kt, w_if, seq):
    t, d = x2.shape
    per_b = seq // TM_IN
    vec = pl.BlockSpec((1, 1, d), lambda i, j: (i // per_b, 0, 0))
    return pl.pallas_call(
        _inproj_kernel,
        grid=(t // TM_IN, Z_W // TN_IN),
        in_specs=[
            pl.BlockSpec((TM_IN, d), lambda i, j: (i, 0)),
            pl.BlockSpec((1, d), lambda i, j: (0, 0)),
            vec, vec,
            pl.BlockSpec((d, TN_IN), lambda i, j: (0, j)),
            pl.BlockSpec((M_QK_W, d), lambda i, j: (0, 0)),
            pl.BlockSpec((GATE_ROWS, d), lambda i, j: (0, 0)),
        ],
        out_specs=[
            pl.BlockSpec((TM_IN, TN_IN), lambda i, j: (i, j)),
            pl.BlockSpec((M_QK_W, TM_IN), lambda i, j: (0, i)),
            pl.BlockSpec((GATE_ROWS, TM_IN), lambda i, j: (0, i)),
        ],
        out_shape=[
            jax.ShapeDtypeStruct((t, Z_W), BF16),
            jax.ShapeDtypeStruct((M_QK_W, t), BF16),
            jax.ShapeDtypeStruct((GATE_ROWS, t), F32),
        ],
        scratch_shapes=[pltpu.VMEM((TM_IN, d), BF16), pltpu.VMEM((TM_IN, LANES), F32)],
        compiler_params=_params(("parallel", "arbitrary")),
        name="inproj",
    )(x2, g.reshape(1, d), scale, shift, w_z, w_kt, w_if)


def _mlstm_kernel(bias_ref, q_ref, kt_ref, v_ref, o_ref, g_ref, gn_ref, out_ref,
                  b_scr, i_scr, ct_scr, nr_scr, num_scr, dn_scr):
    L = M_CHUNK
    nc = g_ref.shape[2]
    scale = M_QK ** -0.5

    lane = lax.broadcasted_iota(jnp.int32, (nc, L), 1)
    for h in range(M_HEADS):
        fg = g_ref[M_HEADS + h, 0] + bias_ref[1, h]
        bc = jnp.minimum(fg, 0.0) - jnp.log(1.0 + jnp.exp(-jnp.abs(fg)))
        sh = 1
        while sh < L:
            bc = bc + jnp.where(lane >= sh, pltpu.roll(bc, sh, axis=1), 0.0)
            sh *= 2
        b_scr[h] = bc
        i_scr[h] = g_ref[h, 0] + bias_ref[0, h]
    ct_scr[...] = jnp.zeros_like(ct_scr)
    nr_scr[...] = jnp.zeros_like(nr_scr)

    rows = lax.broadcasted_iota(jnp.int32, (L, L), 0)
    cols = lax.broadcasted_iota(jnp.int32, (L, L), 1)
    eye = rows == cols
    tril = rows >= cols
    ones = jnp.ones((L, LANES), BF16)

    def to_col(row):
        return jnp.sum(jnp.where(eye, jnp.broadcast_to(row, (L, L)), 0.0), axis=1, keepdims=True)

    def step(c, ms):
        r0 = c * L if isinstance(c, int) else pl.multiple_of(c * L, L)
        heads = range(M_HEADS)
        qb = [q_ref[0, pl.ds(r0, L), h * M_QK:(h + 1) * M_QK] for h in heads]
        kt = [kt_ref[h * M_QK:(h + 1) * M_QK, pl.ds(r0, L)] for h in heads]
        vb = [v_ref[0, pl.ds(r0, L), h * M_V:(h + 1) * M_V] for h in heads]
        b_row = [b_scr[h, pl.ds(c, 1), :] for h in heads]
        i_row = [i_scr[h, pl.ds(c, 1), :] for h in heads]
        ct = [ct_scr[h] for h in heads]
        nr = [nr_scr[h] for h in heads]

        qk = [_dot(qb[h], kt[h]) for h in heads]
        inter = [_dot(qb[h], ct[h].astype(BF16)) for h in heads]
        qn = [_dot(qb[h], nr[h].astype(BF16)) for h in heads]

        m_out = []
        for h in heads:
            b_last = b_row[h][:, L - 1:L]
            g_row = b_last - b_row[h] + i_row[h]
            m_new = jnp.maximum(b_last + ms[h], jnp.max(g_row, axis=1, keepdims=True))
            decay = jnp.exp(b_last + ms[h] - m_new)
            ktw = (kt[h].astype(F32) * (jnp.exp(g_row - m_new) * scale)).astype(BF16)
            ct_scr[h] = decay * ct[h] + _dot(ktw, vb[h])
            nr_scr[h] = decay * nr[h] + _dot(ktw, ones)
            m_out.append(m_new)

        w_inter, e_neg, sb = [], [], []
        for h in heads:
            b_col = to_col(b_row[h])
            a_col = b_col + ms[h]
            dm = jnp.where(tril, b_col - b_row[h] + i_row[h], -jnp.inf)
            m_t = jnp.maximum(a_col, jnp.max(dm, axis=1, keepdims=True))
            w_inter.append(jnp.exp(a_col - m_t))
            e_neg.append(jnp.exp(-m_t))
            sb.append(((qk[h] * scale) * jnp.exp(dm - m_t)).astype(BF16))

        sv = [_dot(sb[h], vb[h]) for h in heads]
        ssum = [_dot(sb[h], ones) for h in heads]

        for h in heads:
            num_scr[h] = w_inter[h] * inter[h] + sv[h]
            den = w_inter[h] * qn[h] + ssum[h]
            dn_scr[h] = jnp.maximum(jnp.abs(den), e_neg[h])
        return tuple(m_out)

    def finish(c):
        r0 = c * L if isinstance(c, int) else pl.multiple_of(c * L, L)
        for h in range(M_HEADS):
            inv = 1.0 / dn_scr[h]
            hh = num_scr[h] * jnp.concatenate([inv] * (M_V // LANES), axis=1)
            o = o_ref[0, pl.ds(r0, L), h * M_V:(h + 1) * M_V].astype(F32)
            out_ref[0, pl.ds(r0, L), h * M_V:(h + 1) * M_V] = (
                (_rms(hh) * gn_ref[h:h + 1, :]) * jax.nn.sigmoid(o)).astype(BF16)

    def pipelined(c, ms):
        finish(c - 1)
        return step(c, ms)

    ms = step(0, tuple(jnp.zeros((1, 1), F32) for _ in range(M_HEADS)))
    lax.fori_loop(1, nc, pipelined, ms)
    finish(nc - 1)


def _mlstm(z3, kt, gates_t, b_mgate, g_mnorm):
    b, s, _ = z3.shape
    nc = s // M_CHUNK
    g4 = gates_t.reshape(GATE_ROWS, b, nc, M_CHUNK)
    return pl.pallas_call(
        _mlstm_kernel,
        grid=(b,),
        in_specs=[
            pl.BlockSpec(memory_space=pltpu.SMEM),
            pl.BlockSpec((1, s, M_QK_W), lambda i: (i, 0, Z_QM // M_QK_W)),
            pl.BlockSpec((M_QK_W, s), lambda i: (0, i)),
            pl.BlockSpec((1, s, M_V_W), lambda i: (i, 0, Z_VM // M_V_W)),
            pl.BlockSpec((1, s, M_V_W), lambda i: (i, 0, Z_OM // M_V_W)),
            pl.BlockSpec((GATE_ROWS, 1, nc, M_CHUNK), lambda i: (0, i, 0, 0)),
            pl.BlockSpec((M_HEADS, M_V), lambda i: (0, 0)),
        ],
        out_specs=pl.BlockSpec((1, s, M_V_W), lambda i: (i, 0, 0)),
        out_shape=jax.ShapeDtypeStruct((b, s, M_V_W), BF16),
        scratch_shapes=[
            pltpu.VMEM((M_HEADS, nc, M_CHUNK), F32), pltpu.VMEM((M_HEADS, nc, M_CHUNK), F32),
            pltpu.VMEM((M_HEADS, M_QK, M_V), F32), pltpu.VMEM((M_HEADS, M_QK, LANES), F32),
            pltpu.VMEM((M_HEADS, M_CHUNK, M_V), F32), pltpu.VMEM((M_HEADS, M_CHUNK, LANES), F32),
        ],
        compiler_params=_params(("parallel",)),
        name="mlstm",
    )(b_mgate, z3, kt, z3, z3, g4, g_mnorm)


def _rope(x, cos_t, sin_t):
    return x * cos_t + pltpu.roll(x, LANES // 2, axis=1) * sin_t


def _mla_proj_kernel(ql_ref, kvl_ref, kr_ref, gq_ref, gkv_ref, wq_ref, wk_ref, wv_ref,
                     gqn_ref, gkn_ref, cos_ref, sin_ref, q_out, k_out, v_out):
    scale = QK_DIM ** -0.5
    qn = (_rms(ql_ref[...].astype(F32)) * gq_ref[...]).astype(BF16)
    kvn = (_rms(kvl_ref[...].astype(F32)) * gkv_ref[...]).astype(BF16)
    qf = _dot(qn, wq_ref[...])
    kf = _dot(kvn, wk_ref[...])
    v_out[...] = _dot(kvn, wv_ref[...]).astype(BF16)

    cos_t = cos_ref[...]
    sin_t = sin_ref[...]
    gqn = gqn_ref[...] * scale
    gkn = gkn_ref[...]
    kr = kr_ref[...].astype(F32)
    kr_ss = jnp.sum(kr * kr, axis=-1, keepdims=True)
    for h in range(MLA_HEADS):
        qh = qf[:, h * QK_PAD:(h + 1) * QK_PAD]
        qh = _rms(qh, QK_DIM) * gqn
        q_out[:, h * QK_PAD:h * QK_PAD + LANES] = qh[:, :LANES].astype(BF16)
        q_out[:, h * QK_PAD + LANES:(h + 1) * QK_PAD] = _rope(qh[:, LANES:], cos_t, sin_t).astype(BF16)

        kh = kf[:, h * NOPE_DIM:(h + 1) * NOPE_DIM]
        ss = jnp.sum(kh * kh, axis=-1, keepdims=True) + kr_ss
        r = lax.rsqrt(ss * (1.0 / QK_DIM) + NORM_EPS)
        k_out[:, h * QK_PAD:h * QK_PAD + LANES] = (kh * r * gkn[:, :LANES]).astype(BF16)
        k_out[:, h * QK_PAD + LANES:(h + 1) * QK_PAD] = _rope(kr * r * gkn[:, LANES:], cos_t, sin_t).astype(BF16)


def _mla_proj(z, g_qlat, g_kvlat, w_q, w_k, w_v, g_qn, g_kn, cos_t, sin_t, seq):
    t = z.shape[0]
    per_b = seq // TM_MLA
    full = lambda shape: pl.BlockSpec(shape, lambda i: (0,) * len(shape))
    return pl.pallas_call(
        _mla_proj_kernel,
        grid=(t // TM_MLA,),
        in_specs=[
            pl.BlockSpec((TM_MLA, Q_RANK), lambda i: (i, Z_QLAT // Q_RANK)),
            pl.BlockSpec((TM_MLA, KV_RANK), lambda i: (i, Z_KVLAT // KV_RANK)),
            pl.BlockSpec((TM_MLA, LANES), lambda i: (i, Z_KR // LANES)),
            full((1, Q_RANK)), full((1, KV_RANK)),
            full(w_q.shape), full(w_k.shape), full(w_v.shape),
            full((1, QK_PAD)), full((1, QK_PAD)),
            pl.BlockSpec((TM_MLA, LANES), lambda i: (i % per_b, 0)),
            pl.BlockSpec((TM_MLA, LANES), lambda i: (i % per_b, 0)),
        ],
        out_specs=[
            pl.BlockSpec((TM_MLA, MLA_HEADS * QK_PAD), lambda i: (i, 0)),
            pl.BlockSpec((TM_MLA, MLA_HEADS * QK_PAD), lambda i: (i, 0)),
            pl.BlockSpec((TM_MLA, MLA_W), lambda i: (i, 0)),
        ],
        out_shape=[
            jax.ShapeDtypeStruct((t, MLA_HEADS * QK_PAD), BF16),
            jax.ShapeDtypeStruct((t, MLA_HEADS * QK_PAD), BF16),
            jax.ShapeDtypeStruct((t, MLA_W), BF16),
        ],
        compiler_params=_params(("parallel",)),
        name="mla_proj",
    )(z, z, z, g_qlat.reshape(1, Q_RANK), g_kvlat.reshape(1, KV_RANK), w_q, w_k, w_v,
      g_qn, g_kn, cos_t, sin_t)


def _attn_kernel(q_ref, k_ref, v_ref, o_ref):
    s = q_ref.shape[0]
    rows = lax.broadcasted_iota(jnp.int32, (TQ, TQ), 0)
    cols = lax.broadcasted_iota(jnp.int32, (TQ, TQ), 1)
    causal = rows >= cols
    for blk in range(s // TQ):
        s0, s1 = blk * TQ, (blk + 1) * TQ
        q = q_ref[s0:s1, :]
        diag = jnp.where(causal, _dot_nt(q, k_ref[s0:s1, :]), -jnp.inf)
        m = jnp.max(diag, axis=-1, keepdims=True)
        if blk:
            past = _dot_nt(q, k_ref[:s0, :])
            m = jnp.maximum(m, jnp.max(past, axis=-1, keepdims=True))
        p = jnp.exp(diag - m)
        denom = jnp.sum(p, axis=-1, keepdims=True)
        acc = _dot(p.astype(BF16), v_ref[s0:s1, :])
        if blk:
            pp = jnp.exp(past - m)
            denom = denom + jnp.sum(pp, axis=-1, keepdims=True)
            acc = acc + _dot(pp.astype(BF16), v_ref[:s0, :])
        o_ref[s0:s1, :] = (acc * (1.0 / denom)).astype(BF16)


def _attention(q, k, v, batch, seq):
    t = q.shape[0]
    return pl.pallas_call(
        _attn_kernel,
        grid=(batch, MLA_HEADS),
        in_specs=[
            pl.BlockSpec((seq, QK_PAD), lambda b, h: (b, h)),
            pl.BlockSpec((seq, QK_PAD), lambda b, h: (b, h)),
            pl.BlockSpec((seq, MLA_V), lambda b, h: (b, h)),
        ],
        out_specs=pl.BlockSpec((seq, MLA_V), lambda b, h: (b, h)),
        out_shape=jax.ShapeDtypeStruct((t, MLA_W), BF16),
        compiler_params=_params(("parallel", "parallel")),
        name="attention",
    )(q, k, v)


def _pool_kernel(u0_ref, u1_ref, u2_ref, u3_ref, w_ref, b_ref, s_ref, o_ref):
    s = o_ref.shape[0]
    row = lax.broadcasted_iota(jnp.int32, (s, 1), 0)
    for g, (u_ref, win) in enumerate(zip((u0_ref, u1_ref, u2_ref, u3_ref), POOL_WINDOWS)):
        c0, c1 = g * POOL_GROUP_W, (g + 1) * POOL_GROUP_W
        u = u_ref[...].astype(F32)
        acc = u
        sh = 1
        while sh < win:
            acc = acc + jnp.where(row >= sh, pltpu.roll(acc, sh, axis=0), 0.0)
            sh *= 2
        inv_cnt = 1.0 / jnp.minimum(row + 1, win).astype(F32)
        pooled = (acc * inv_cnt - u).astype(BF16)
        y = _dot(pooled, w_ref[g]) + b_ref[:, c0:c1]
        o_ref[:, c0:c1] = (y * s_ref[:, c0:c1]).astype(BF16)


def _pool(z, w_pool, layer, b_pool, s_pool, batch, seq):
    t = z.shape[0]
    group = lambda g: pl.BlockSpec((seq, POOL_GROUP_W), lambda b: (b, Z_POOL // POOL_GROUP_W + g))
    return pl.pallas_call(
        _pool_kernel,
        grid=(batch,),
        in_specs=[
            group(0), group(1), group(2), group(3),
            pl.BlockSpec((None,) + w_pool.shape[1:], lambda b: (layer, 0, 0, 0)),
            pl.BlockSpec((1, POOL_W), lambda b: (0, 0)),
            pl.BlockSpec((1, POOL_W), lambda b: (0, 0)),
        ],
        out_specs=pl.BlockSpec((seq, POOL_W), lambda b: (b, 0)),
        out_shape=jax.ShapeDtypeStruct((t, POOL_W), BF16),
        compiler_params=_params(("parallel",)),
        name="pool",
    )(z, z, z, z, w_pool, b_pool.reshape(1, POOL_W), s_pool.reshape(1, POOL_W))


def _merge_kernel(a_ref, b_ref, c_ref, ga_ref, gb_ref, gc_ref, wb_ref, wo_ref, x_ref, gate_ref,
                  o_ref):
    merged = (jax.nn.sigmoid(ga_ref[...].astype(F32)) * _dot(a_ref[...], wb_ref[0])
              + jax.nn.sigmoid(gb_ref[...].astype(F32)) * _dot(b_ref[...], wb_ref[1])
              + jax.nn.sigmoid(gc_ref[...].astype(F32)) * _dot(c_ref[...], wb_ref[2]))
    o_ref[...] = x_ref[...] + gate_ref[0] * _dot(merged.astype(BF16), wo_ref[...])


def _merge(br_a, br_b, br_c, z, w_branch, w_out, layer, x2, gate1, seq):
    t, d = x2.shape
    per_b = seq // TM_MERGE
    g0 = Z_GATES // d
    br = pl.BlockSpec((TM_MERGE, BRANCH_W), lambda i: (i, 0))
    gate = lambda k: pl.BlockSpec((TM_MERGE, d), lambda i: (i, g0 + k))
    return pl.pallas_call(
        _merge_kernel,
        grid=(t // TM_MERGE,),
        in_specs=[
            br, br, br, gate(0), gate(1), gate(2),
            pl.BlockSpec((None, N_BRANCH, BRANCH_W, d), lambda i: (layer, 0, 0, 0),
                         pipeline_mode=pl.Buffered(1)),
            pl.BlockSpec((None, d, d), lambda i: (layer, 0, 0), pipeline_mode=pl.Buffered(1)),
            pl.BlockSpec((TM_MERGE, d), lambda i: (i, 0)),
            pl.BlockSpec((1, 1, d), lambda i: (i // per_b, 0, 0)),
        ],
        out_specs=pl.BlockSpec((TM_MERGE, d), lambda i: (i, 0)),
        out_shape=jax.ShapeDtypeStruct((t, d), F32),
        compiler_params=_params(("parallel",)),
        name="merge",
    )(br_a, br_b, br_c, z, z, z, w_branch, w_out, x2, gate1)


def _ffn_kernel(x_ref, g_ref, sc_ref, sh_ref, gate_ref, wg_ref, wu_ref, wo_ref, o_ref,
                h_scr, r_scr):
    j = pl.program_id(1)

    @pl.when(j == 0)
    def _():
        _norm_modulate_into(x_ref, g_ref, sc_ref, sh_ref, h_scr, r_scr)
        o_ref[...] = jnp.zeros_like(o_ref)

    for r0 in range(0, TM_FFN, TM_FFN_SUB):
        h = h_scr[r0:r0 + TM_FFN_SUB, :]
        gp = _dot(h, wg_ref[...])
        up = _dot(h, wu_ref[...])
        act = ((gp * jax.nn.sigmoid(gp)) * up).astype(BF16)
        o_ref[r0:r0 + TM_FFN_SUB, :] += _dot(act, wo_ref[...])

    @pl.when(j == pl.num_programs(1) - 1)
    def _():
        o_ref[...] = x_ref[...] + gate_ref[0] * o_ref[...]


def _ffn(x2, g, scale, shift, gate, w_in, w_out, layer, seq):
    t, d = x2.shape
    per_b = seq // TM_FFN
    nj = FFN_DIM // TF_FFN
    vec = pl.BlockSpec((1, 1, d), lambda i, j: (i // per_b, 0, 0))
    return pl.pallas_call(
        _ffn_kernel,
        grid=(t // TM_FFN, nj),
        in_specs=[
            pl.BlockSpec((TM_FFN, d), lambda i, j: (i, 0)),
            pl.BlockSpec((1, d), lambda i, j: (0, 0)),
            vec, vec, vec,
            pl.BlockSpec((None, d, TF_FFN), lambda i, j: (layer, 0, j)),
            pl.BlockSpec((None, d, TF_FFN), lambda i, j: (layer, 0, nj + j)),
            pl.BlockSpec((None, TF_FFN, d), lambda i, j: (layer, j, 0)),
        ],
        out_specs=pl.BlockSpec((TM_FFN, d), lambda i, j: (i, 0)),
        out_shape=jax.ShapeDtypeStruct((t, d), F32),
        scratch_shapes=[pltpu.VMEM((TM_FFN, d), BF16), pltpu.VMEM((TM_FFN, LANES), F32)],
        compiler_params=_params(("parallel", "arbitrary")),
        name="ffn",
    )(x2, g.reshape(1, d), scale, shift, gate, w_in, w_in, w_out)


def _rope_lanes(a, axis):
    x1, x2 = jnp.split(a, 2, axis=axis)
    zero = jnp.zeros_like(x1)
    return jnp.concatenate([x1, zero, x2, zero], axis=axis)


def _prep_w_in(w):
    d = w.shape[0]
    parts = []
    off = 0
    for size in IN_SIZES:
        parts.append(w[:, off:off + size])
        off += size
    q_m, k_m, v_m, o_m, i_m, f_m, q_lat, kv_lat, k_r, u_pool, gates = parts
    w_z = jnp.concatenate(
        [gates, q_m, q_lat, v_m, o_m, kv_lat, _rope_lanes(k_r, 1), jnp.zeros((d, LANES), w.dtype),
         u_pool], axis=1).astype(BF16)
    w_kt = k_m.T.astype(BF16)
    w_if = jnp.concatenate(
        [i_m, f_m, jnp.zeros((d, GATE_ROWS - 2 * M_HEADS), w.dtype)], axis=1).T.astype(BF16)
    return w_z, w_kt, w_if


def _prep_head_cols(a):
    return jnp.concatenate([a[..., :NOPE_DIM], _rope_lanes(a[..., NOPE_DIM:], -1)], axis=-1)


def _rope_tables(seq):
    pos = jnp.arange(seq, dtype=F32)
    freqs = ROPE_THETA ** (-jnp.arange(0, ROPE_DIM, 2, dtype=F32) / ROPE_DIM)
    ang = pos[:, None] * freqs[None, :]
    cos, sin = jnp.cos(ang), jnp.sin(ang)
    zero = jnp.zeros_like(cos)
    cos_t = jnp.concatenate([cos, zero, cos, zero], axis=1)
    sin_t = jnp.concatenate([-sin, zero, sin, zero], axis=1)
    return cos_t, sin_t


def kernel(x, c, w_ada, b_ada, g_norm1, w_in, b_mgate, g_mnorm, g_qlat, w_uq, g_kvlat, w_ukv,
           g_qn, g_kn, w_pool, b_pool, s_pool, w_branch, w_out, g_norm2, w_ffn_in, w_ffn_out):
    batch, seq, d = x.shape
    depth = w_in.shape[0]
    t = batch * seq

    mod = _ada(c, w_ada, b_ada)
    cos_t, sin_t = _rope_tables(seq)
    x2 = x.reshape(t, d)
    w_pool_b, w_branch_b, w_out_b = w_pool.astype(BF16), w_branch.astype(BF16), w_out.astype(BF16)
    w_ffn_in_b, w_ffn_out_b = w_ffn_in.astype(BF16), w_ffn_out.astype(BF16)

    for l in range(depth):
        shift1, scale1, gate1, shift2, scale2, gate2 = [
            mod[l, :, k * d:(k + 1) * d].reshape(batch, 1, d) for k in range(6)]

        w_z, w_kt, w_if = _prep_w_in(w_in[l])
        z, kt, gates_t = _inproj(x2, g_norm1[l], scale1, shift1, w_z, w_kt, w_if, seq)

        br_a = _mlstm(z.reshape(batch, seq, Z_W), kt, gates_t, b_mgate[l],
                      g_mnorm[l]).reshape(t, M_V_W)

        w_q = _prep_head_cols(w_uq[l].reshape(Q_RANK, MLA_HEADS, QK_DIM)).reshape(
            Q_RANK, MLA_HEADS * QK_PAD).astype(BF16)
        w_kv = w_ukv[l].reshape(KV_RANK, MLA_HEADS, NOPE_DIM + MLA_V)
        w_k = w_kv[..., :NOPE_DIM].reshape(KV_RANK, MLA_HEADS * NOPE_DIM).astype(BF16)
        w_v = w_kv[..., NOPE_DIM:].reshape(KV_RANK, MLA_W).astype(BF16)
        q, k, v = _mla_proj(z, g_qlat[l], g_kvlat[l], w_q, w_k, w_v,
                            _prep_head_cols(g_qn[l]).reshape(1, QK_PAD),
                            _prep_head_cols(g_kn[l]).reshape(1, QK_PAD), cos_t, sin_t, seq)
        br_b = _attention(q, k, v, batch, seq)

        br_c = _pool(z, w_pool_b, l, b_pool[l], s_pool[l], batch, seq)

        x2 = _merge(br_a, br_b, br_c, z, w_branch_b, w_out_b, l, x2, gate1, seq)
        x2 = _ffn(x2, g_norm2[l], scale2, shift2, gate2, w_ffn_in_b, w_ffn_out_b, l, seq)

    return x2.reshape(batch, seq, d)
```

```python
import functools

import jax
import jax.numpy as jnp
from jax import lax
from jax.experimental import pallas as pl
from jax.experimental.pallas import tpu as pltpu

F32 = jnp.float32
BF16 = jnp.bfloat16

D_MODEL = 2048
M_HEADS = 4
M_QK = 128
M_V = 256
M_QK_W = M_HEADS * M_QK
M_V_W = M_HEADS * M_V
M_CHUNK = 128
MLA_HEADS = 8
NOPE_DIM = 128
ROPE_DIM = 64
ROPE_HALF = ROPE_DIM // 2
QK_DIM = NOPE_DIM + ROPE_DIM
MLA_V = 128
MLA_W = MLA_HEADS * MLA_V
Q_RANK = 512
KV_RANK = 256
ROPE_THETA = 10000.0
POOL_WINDOWS = (2, 4, 8, 16)
POOL_GROUPS = 4
POOL_GROUP_W = 256
POOL_W = POOL_GROUPS * POOL_GROUP_W
N_BRANCH = 3
BRANCH_W = 1024
FFN_DIM = ((8 * D_MODEL // 3 + 255) // 256) * 256
NORM_EPS = 1e-6
IN_SIZES = (M_QK_W, M_QK_W, M_V_W, M_V_W, M_HEADS, M_HEADS, Q_RANK, KV_RANK, ROPE_DIM, POOL_W,
            N_BRANCH * D_MODEL)

LANES = 128
QK_PAD = 2 * LANES
GATE_ROWS = 16

Z_GATES = 0
Z_QM = Z_GATES + N_BRANCH * D_MODEL
Z_QLAT = Z_QM + M_QK_W
Z_VM = Z_QLAT + Q_RANK
Z_OM = Z_VM + M_V_W
Z_KVLAT = Z_OM + M_V_W
Z_KR = Z_KVLAT + KV_RANK
Z_POOL = Z_KR + 2 * LANES
Z_W = Z_POOL + POOL_W

V7X_VMEM_LIMIT = 52 * 1024 * 1024

TM_IN = 1024
TN_IN = 1536
NORM_ROWS = 64
NORM_UNROLL = 4
TM_MLA = 512
TQ = 256
TM_MERGE = 256
TM_FFN = 1024
TM_FFN_SUB = 512
TF_FFN = 512
TN_ADA = 1024


def _params(semantics):
    return pltpu.CompilerParams(dimension_semantics=semantics, vmem_limit_bytes=V7X_VMEM_LIMIT)


def _rms(x, width=None):
    ss = jnp.sum(x * x, axis=-1, keepdims=True)
    n = x.shape[-1] if width is None else width
    return x * lax.rsqrt(ss * (1.0 / n) + NORM_EPS)


def _dot(a, b):
    return jnp.dot(a, b, preferred_element_type=F32)


def _dot_nt(a, b):
    return lax.dot_general(a, b, (((1,), (1,)), ((), ())), preferred_element_type=F32)


def _ada_kernel(c_ref, w_ref, b_ref, o_ref):
    c = c_ref[...]
    ca = (c * jax.nn.sigmoid(c)).astype(BF16)
    o_ref[0] = _dot(ca, w_ref[0].astype(BF16)) + b_ref[0]


def _ada(c, w_ada, b_ada):
    depth, d, n = w_ada.shape
    b = c.shape[0]
    return pl.pallas_call(
        _ada_kernel,
        grid=(depth, n // TN_ADA),
        in_specs=[
            pl.BlockSpec((b, d), lambda l, j: (0, 0)),
            pl.BlockSpec((1, d, TN_ADA), lambda l, j: (l, 0, j)),
            pl.BlockSpec((1, 1, TN_ADA), lambda l, j: (l, 0, j)),
        ],
        out_specs=pl.BlockSpec((1, b, TN_ADA), lambda l, j: (l, 0, j)),
        out_shape=jax.ShapeDtypeStruct((depth, b, n), F32),
        compiler_params=_params(("parallel", "parallel")),
        name="adaln",
    )(c, w_ada, b_ada.reshape(depth, 1, n))


def _norm_modulate_into(x_ref, g_ref, sc_ref, sh_ref, h_scr, r_scr):
    n_slabs = x_ref.shape[0] // NORM_ROWS
    d = x_ref.shape[1]

    def stats(r, carry):
        r0 = pl.multiple_of(r * NORM_ROWS, NORM_ROWS)
        x = x_ref[pl.ds(r0, NORM_ROWS), :]
        ms = jnp.sum(x * x, axis=-1, keepdims=True) * (1.0 / d)
        r_scr[pl.ds(r0, NORM_ROWS), :] = jnp.broadcast_to(lax.rsqrt(ms + NORM_EPS), (NORM_ROWS, LANES))
        return carry

    lax.fori_loop(0, n_slabs, stats, 0, unroll=NORM_UNROLL)

    gain = g_ref[...] * (1.0 + sc_ref[0])
    shift = sh_ref[0]

    def apply(r, carry):
        r0 = pl.multiple_of(r * NORM_ROWS, NORM_ROWS)
        rstd = r_scr[pl.ds(r0, NORM_ROWS), :]
        for t in range(d // LANES):
            c0, c1 = t * LANES, (t + 1) * LANES
            x = x_ref[pl.ds(r0, NORM_ROWS), c0:c1]
            h_scr[pl.ds(r0, NORM_ROWS), c0:c1] = (x * rstd * gain[:, c0:c1] + shift[:, c0:c1]).astype(BF16)
        return carry

    lax.fori_loop(0, n_slabs, apply, 0)


def _inproj_kernel(x_ref, g_ref, sc_ref, sh_ref, w_ref, wkt_ref, wif_ref, z_ref, kt_ref, gt_ref,
                   h_scr, r_scr):
    @pl.when(pl.program_id(1) == 0)
    def _():
        _norm_modulate_into(x_ref, g_ref, sc_ref, sh_ref, h_scr, r_scr)
        h = h_scr[...]
        kt_ref[...] = _dot_nt(wkt_ref[...], h).astype(BF16)
        gt_ref[...] = _dot_nt(wif_ref[...], h)

    z_ref[...] = _dot(h_scr[...], w_ref[...]).astype(BF16)


def _inproj(x2, g, scale, shift, w_z, layer, w_kt, w_if, seq):
    t, d = x2.shape
    per_b = seq // TM_IN
    vec = pl.BlockSpec((1, 1, d), lambda i, j: (i // per_b, 0, 0))
    return pl.pallas_call(
        _inproj_kernel,
        grid=(t // TM_IN, Z_W // TN_IN),
        in_specs=[
            pl.BlockSpec((TM_IN, d), lambda i, j: (i, 0)),
            pl.BlockSpec((1, d), lambda i, j: (0, 0)),
            vec, vec,
            pl.BlockSpec((None, d, TN_IN), lambda i, j: (layer, 0, j)),
            pl.BlockSpec((M_QK_W, d), lambda i, j: (0, 0)),
            pl.BlockSpec((GATE_ROWS, d), lambda i, j: (0, 0)),
        ],
        out_specs=[
            pl.BlockSpec((TM_IN, TN_IN), lambda i, j: (i, j)),
            pl.BlockSpec((M_QK_W, TM_IN), lambda i, j: (0, i)),
            pl.BlockSpec((GATE_ROWS, TM_IN), lambda i, j: (0, i)),
        ],
        out_shape=[
            jax.ShapeDtypeStruct((t, Z_W), BF16),
            jax.ShapeDtypeStruct((M_QK_W, t), BF16),
            jax.ShapeDtypeStruct((GATE_ROWS, t), F32),
        ],
        scratch_shapes=[pltpu.VMEM((TM_IN, d), BF16), pltpu.VMEM((TM_IN, LANES), F32)],
        compiler_params=_params(("parallel", "arbitrary")),
        name="inproj",
    )(x2, g.reshape(1, d), scale, shift, w_z, w_kt, w_if)


def _mlstm_kernel(bias_ref, q_ref, kt_ref, v_ref, o_ref, g_ref, gn_ref, out_ref,
                  b_scr, i_scr, ct_scr, nr_scr, num_scr, dn_scr):
    L = M_CHUNK
    nc = g_ref.shape[2]
    scale = M_QK ** -0.5

    lane = lax.broadcasted_iota(jnp.int32, (nc, L), 1)
    for h in range(M_HEADS):
        fg = g_ref[M_HEADS + h, 0] + bias_ref[1, h]
        bc = jnp.minimum(fg, 0.0) - jnp.log(1.0 + jnp.exp(-jnp.abs(fg)))
        sh = 1
        while sh < L:
            bc = bc + jnp.where(lane >= sh, pltpu.roll(bc, sh, axis=1), 0.0)
            sh *= 2
        b_scr[h] = bc
        i_scr[h] = g_ref[h, 0] + bias_ref[0, h]
    ct_scr[...] = jnp.zeros_like(ct_scr)
    nr_scr[...] = jnp.zeros_like(nr_scr)

    rows = lax.broadcasted_iota(jnp.int32, (L, L), 0)
    cols = lax.broadcasted_iota(jnp.int32, (L, L), 1)
    eye = rows == cols
    tril = rows >= cols
    ones = jnp.ones((L, LANES), BF16)

    def to_col(row):
        return jnp.sum(jnp.where(eye, jnp.broadcast_to(row, (L, L)), 0.0), axis=1, keepdims=True)

    def step(c, ms):
        r0 = c * L if isinstance(c, int) else pl.multiple_of(c * L, L)
        heads = range(M_HEADS)
        qb = [q_ref[0, pl.ds(r0, L), h * M_QK:(h + 1) * M_QK] for h in heads]
        kt = [kt_ref[h * M_QK:(h + 1) * M_QK, pl.ds(r0, L)] for h in heads]
        vb = [v_ref[0, pl.ds(r0, L), h * M_V:(h + 1) * M_V] for h in heads]
        b_row = [b_scr[h, pl.ds(c, 1), :] for h in heads]
        i_row = [i_scr[h, pl.ds(c, 1), :] for h in heads]
        ct = [ct_scr[h] for h in heads]
        nr = [nr_scr[h] for h in heads]

        qk = [_dot(qb[h], kt[h]) for h in heads]
        inter = [_dot(qb[h], ct[h].astype(BF16)) for h in heads]
        qn = [_dot(qb[h], nr[h].astype(BF16)) for h in heads]

        m_out = []
        for h in heads:
            b_last = b_row[h][:, L - 1:L]
            g_row = b_last - b_row[h] + i_row[h]
            m_new = jnp.maximum(b_last + ms[h], jnp.max(g_row, axis=1, keepdims=True))
            decay = jnp.exp(b_last + ms[h] - m_new)
            ktw = (kt[h].astype(F32) * (jnp.exp(g_row - m_new) * scale)).astype(BF16)
            ct_scr[h] = decay * ct[h] + _dot(ktw, vb[h])
            nr_scr[h] = decay * nr[h] + _dot(ktw, ones)
            m_out.append(m_new)

        w_inter, e_neg, sb = [], [], []
        for h in heads:
            b_col = to_col(b_row[h])
            a_col = b_col + ms[h]
            dm = jnp.where(tril, b_col - b_row[h] + i_row[h], -jnp.inf)
            m_t = jnp.maximum(a_col, jnp.max(dm, axis=1, keepdims=True))
            w_inter.append(jnp.exp(a_col - m_t))
            e_neg.append(jnp.exp(-m_t))
            sb.append(((qk[h] * scale) * jnp.exp(dm - m_t)).astype(BF16))

        sv = [_dot(sb[h], vb[h]) for h in heads]
        ssum = [_dot(sb[h], ones) for h in heads]

        for h in heads:
            num_scr[h] = w_inter[h] * inter[h] + sv[h]
            den = w_inter[h] * qn[h] + ssum[h]
            dn_scr[h] = jnp.maximum(jnp.abs(den), e_neg[h])
        return tuple(m_out)

    def finish(c):
        r0 = c * L if isinstance(c, int) else pl.multiple_of(c * L, L)
        for h in range(M_HEADS):
            inv = 1.0 / dn_scr[h]
            hh = num_scr[h] * jnp.concatenate([inv] * (M_V // LANES), axis=1)
            o = o_ref[0, pl.ds(r0, L), h * M_V:(h + 1) * M_V].astype(F32)
            out_ref[0, pl.ds(r0, L), h * M_V:(h + 1) * M_V] = (
                (_rms(hh) * gn_ref[h:h + 1, :]) * jax.nn.sigmoid(o)).astype(BF16)

    def pipelined(c, ms):
        finish(c - 1)
        return step(c, ms)

    ms = step(0, tuple(jnp.zeros((1, 1), F32) for _ in range(M_HEADS)))
    lax.fori_loop(1, nc, pipelined, ms)
    finish(nc - 1)


def _mlstm(z3, kt, gates_t, b_mgate, g_mnorm):
    b, s, _ = z3.shape
    nc = s // M_CHUNK
    g4 = gates_t.reshape(GATE_ROWS, b, nc, M_CHUNK)
    return pl.pallas_call(
        _mlstm_kernel,
        grid=(b,),
        in_specs=[
            pl.BlockSpec(memory_space=pltpu.SMEM),
            pl.BlockSpec((1, s, M_QK_W), lambda i: (i, 0, Z_QM // M_QK_W)),
            pl.BlockSpec((M_QK_W, s), lambda i: (0, i)),
            pl.BlockSpec((1, s, M_V_W), lambda i: (i, 0, Z_VM // M_V_W)),
            pl.BlockSpec((1, s, M_V_W), lambda i: (i, 0, Z_OM // M_V_W)),
            pl.BlockSpec((GATE_ROWS, 1, nc, M_CHUNK), lambda i: (0, i, 0, 0)),
            pl.BlockSpec((M_HEADS, M_V), lambda i: (0, 0)),
        ],
        out_specs=pl.BlockSpec((1, s, M_V_W), lambda i: (i, 0, 0)),
        out_shape=jax.ShapeDtypeStruct((b, s, M_V_W), BF16),
        scratch_shapes=[
            pltpu.VMEM((M_HEADS, nc, M_CHUNK), F32), pltpu.VMEM((M_HEADS, nc, M_CHUNK), F32),
            pltpu.VMEM((M_HEADS, M_QK, M_V), F32), pltpu.VMEM((M_HEADS, M_QK, LANES), F32),
            pltpu.VMEM((M_HEADS, M_CHUNK, M_V), F32), pltpu.VMEM((M_HEADS, M_CHUNK, LANES), F32),
        ],
        compiler_params=_params(("parallel",)),
        name="mlstm",
    )(b_mgate, z3, kt, z3, z3, g4, g_mnorm)


def _rope(x, cos_t, sin_t):
    return x * cos_t + pltpu.roll(x, LANES // 2, axis=1) * sin_t


def _mla_proj_kernel(ql_ref, kvl_ref, kr_ref, gq_ref, gkv_ref, wq_ref, wk_ref, wv_ref,
                     gqn_ref, gkn_ref, cos_ref, sin_ref, q_out, k_out, v_out):
    scale = QK_DIM ** -0.5
    qn = (_rms(ql_ref[...].astype(F32)) * gq_ref[...]).astype(BF16)
    kvn = (_rms(kvl_ref[...].astype(F32)) * gkv_ref[...]).astype(BF16)
    qf = _dot(qn, wq_ref[...])
    kf = _dot(kvn, wk_ref[...])
    v_out[...] = _dot(kvn, wv_ref[...]).astype(BF16)

    cos_t = cos_ref[...]
    sin_t = sin_ref[...]
    gqn = gqn_ref[...] * scale
    gkn = gkn_ref[...]
    kr = kr_ref[...].astype(F32)
    kr_ss = jnp.sum(kr * kr, axis=-1, keepdims=True)
    for h in range(MLA_HEADS):
        qh = qf[:, h * QK_PAD:(h + 1) * QK_PAD]
        qh = _rms(qh, QK_DIM) * gqn
        q_out[:, h * QK_PAD:h * QK_PAD + LANES] = qh[:, :LANES].astype(BF16)
        q_out[:, h * QK_PAD + LANES:(h + 1) * QK_PAD] = _rope(qh[:, LANES:], cos_t, sin_t).astype(BF16)

        kh = kf[:, h * NOPE_DIM:(h + 1) * NOPE_DIM]
        ss = jnp.sum(kh * kh, axis=-1, keepdims=True) + kr_ss
        r = lax.rsqrt(ss * (1.0 / QK_DIM) + NORM_EPS)
        k_out[:, h * QK_PAD:h * QK_PAD + LANES] = (kh * r * gkn[:, :LANES]).astype(BF16)
        k_out[:, h * QK_PAD + LANES:(h + 1) * QK_PAD] = _rope(kr * r * gkn[:, LANES:], cos_t, sin_t).astype(BF16)


def _mla_proj(z, g_qlat, g_kvlat, w_q, w_k, w_v, g_qn, g_kn, cos_t, sin_t, seq):
    t = z.shape[0]
    per_b = seq // TM_MLA
    full = lambda shape: pl.BlockSpec(shape, lambda i: (0,) * len(shape))
    return pl.pallas_call(
        _mla_proj_kernel,
        grid=(t // TM_MLA,),
        in_specs=[
            pl.BlockSpec((TM_MLA, Q_RANK), lambda i: (i, Z_QLAT // Q_RANK)),
            pl.BlockSpec((TM_MLA, KV_RANK), lambda i: (i, Z_KVLAT // KV_RANK)),
            pl.BlockSpec((TM_MLA, LANES), lambda i: (i, Z_KR // LANES)),
            full((1, Q_RANK)), full((1, KV_RANK)),
            full(w_q.shape), full(w_k.shape), full(w_v.shape),
            full((1, QK_PAD)), full((1, QK_PAD)),
            pl.BlockSpec((TM_MLA, LANES), lambda i: (i % per_b, 0)),
            pl.BlockSpec((TM_MLA, LANES), lambda i: (i % per_b, 0)),
        ],
        out_specs=[
            pl.BlockSpec((TM_MLA, MLA_HEADS * QK_PAD), lambda i: (i, 0)),
            pl.BlockSpec((TM_MLA, MLA_HEADS * QK_PAD), lambda i: (i, 0)),
            pl.BlockSpec((TM_MLA, MLA_W), lambda i: (i, 0)),
        ],
        out_shape=[
            jax.ShapeDtypeStruct((t, MLA_HEADS * QK_PAD), BF16),
            jax.ShapeDtypeStruct((t, MLA_HEADS * QK_PAD), BF16),
            jax.ShapeDtypeStruct((t, MLA_W), BF16),
        ],
        compiler_params=_params(("parallel",)),
        name="mla_proj",
    )(z, z, z, g_qlat.reshape(1, Q_RANK), g_kvlat.reshape(1, KV_RANK), w_q, w_k, w_v,
      g_qn, g_kn, cos_t, sin_t)


def _attn_kernel(q_ref, k_ref, v_ref, o_ref):
    s = q_ref.shape[0]
    rows = lax.broadcasted_iota(jnp.int32, (TQ, TQ), 0)
    cols = lax.broadcasted_iota(jnp.int32, (TQ, TQ), 1)
    causal = rows >= cols

    def logits(blk):
        s0, s1 = blk * TQ, (blk + 1) * TQ
        q = q_ref[s0:s1, :]
        diag = _dot_nt(q, k_ref[s0:s1, :])
        past = _dot_nt(q, k_ref[:s0, :]) if blk else None
        return diag, past

    n_blk = s // TQ
    nxt = logits(0)
    for blk in range(n_blk):
        s0, s1 = blk * TQ, (blk + 1) * TQ
        diag, past = nxt
        if blk + 1 < n_blk:
            nxt = logits(blk + 1)
        diag = jnp.where(causal, diag, -jnp.inf)
        m = jnp.max(diag, axis=-1, keepdims=True)
        if blk:
            m = jnp.maximum(m, jnp.max(past, axis=-1, keepdims=True))
        p = jnp.exp(diag - m)
        denom = jnp.sum(p, axis=-1, keepdims=True)
        acc = _dot(p.astype(BF16), v_ref[s0:s1, :])
        if blk:
            pp = jnp.exp(past - m)
            denom = denom + jnp.sum(pp, axis=-1, keepdims=True)
            acc = acc + _dot(pp.astype(BF16), v_ref[:s0, :])
        o_ref[s0:s1, :] = (acc * (1.0 / denom)).astype(BF16)


def _attention(q, k, v, batch, seq):
    t = q.shape[0]
    return pl.pallas_call(
        _attn_kernel,
        grid=(batch, MLA_HEADS),
        in_specs=[
            pl.BlockSpec((seq, QK_PAD), lambda b, h: (b, h)),
            pl.BlockSpec((seq, QK_PAD), lambda b, h: (b, h)),
            pl.BlockSpec((seq, MLA_V), lambda b, h: (b, h)),
        ],
        out_specs=pl.BlockSpec((seq, MLA_V), lambda b, h: (b, h)),
        out_shape=jax.ShapeDtypeStruct((t, MLA_W), BF16),
        compiler_params=_params(("parallel", "parallel")),
        name="attention",
    )(q, k, v)


def _pool_kernel(u0_ref, u1_ref, u2_ref, u3_ref, w_ref, b_ref, s_ref, o_ref):
    s = o_ref.shape[0]
    row = lax.broadcasted_iota(jnp.int32, (s, 1), 0)
    for g, (u_ref, win) in enumerate(zip((u0_ref, u1_ref, u2_ref, u3_ref), POOL_WINDOWS)):
        c0, c1 = g * POOL_GROUP_W, (g + 1) * POOL_GROUP_W
        u = u_ref[...].astype(F32)
        acc = u
        sh = 1
        while sh < win:
            acc = acc + jnp.where(row >= sh, pltpu.roll(acc, sh, axis=0), 0.0)
            sh *= 2
        inv_cnt = 1.0 / jnp.minimum(row + 1, win).astype(F32)
        pooled = (acc * inv_cnt - u).astype(BF16)
        y = _dot(pooled, w_ref[g]) + b_ref[:, c0:c1]
        o_ref[:, c0:c1] = (y * s_ref[:, c0:c1]).astype(BF16)


def _pool(z, w_pool, layer, b_pool, s_pool, batch, seq):
    t = z.shape[0]
    group = lambda g: pl.BlockSpec((seq, POOL_GROUP_W), lambda b: (b, Z_POOL // POOL_GROUP_W + g))
    return pl.pallas_call(
        _pool_kernel,
        grid=(batch,),
        in_specs=[
            group(0), group(1), group(2), group(3),
            pl.BlockSpec((None,) + w_pool.shape[1:], lambda b: (layer, 0, 0, 0)),
            pl.BlockSpec((1, POOL_W), lambda b: (0, 0)),
            pl.BlockSpec((1, POOL_W), lambda b: (0, 0)),
        ],
        out_specs=pl.BlockSpec((seq, POOL_W), lambda b: (b, 0)),
        out_shape=jax.ShapeDtypeStruct((t, POOL_W), BF16),
        compiler_params=_params(("parallel",)),
        name="pool",
    )(z, z, z, z, w_pool, b_pool.reshape(1, POOL_W), s_pool.reshape(1, POOL_W))


def _merge_kernel(a_ref, b_ref, c_ref, ga_ref, gb_ref, gc_ref, wb_ref, wo_ref, x_ref, gate_ref,
                  o_ref):
    merged = (jax.nn.sigmoid(ga_ref[...].astype(F32)) * _dot(a_ref[...], wb_ref[0])
              + jax.nn.sigmoid(gb_ref[...].astype(F32)) * _dot(b_ref[...], wb_ref[1])
              + jax.nn.sigmoid(gc_ref[...].astype(F32)) * _dot(c_ref[...], wb_ref[2]))
    o_ref[...] = x_ref[...] + gate_ref[0] * _dot(merged.astype(BF16), wo_ref[...])


def _merge(br_a, br_b, br_c, z, w_branch, w_out, layer, x2, gate1, seq):
    t, d = x2.shape
    per_b = seq // TM_MERGE
    g0 = Z_GATES // d
    br = pl.BlockSpec((TM_MERGE, BRANCH_W), lambda i: (i, 0))
    gate = lambda k: pl.BlockSpec((TM_MERGE, d), lambda i: (i, g0 + k))
    return pl.pallas_call(
        _merge_kernel,
        grid=(t // TM_MERGE,),
        in_specs=[
            br, br, br, gate(0), gate(1), gate(2),
            pl.BlockSpec((None, N_BRANCH, BRANCH_W, d), lambda i: (layer, 0, 0, 0),
                         pipeline_mode=pl.Buffered(1)),
            pl.BlockSpec((None, d, d), lambda i: (layer, 0, 0), pipeline_mode=pl.Buffered(1)),
            pl.BlockSpec((TM_MERGE, d), lambda i: (i, 0)),
            pl.BlockSpec((1, 1, d), lambda i: (i // per_b, 0, 0)),
        ],
        out_specs=pl.BlockSpec((TM_MERGE, d), lambda i: (i, 0)),
        out_shape=jax.ShapeDtypeStruct((t, d), F32),
        compiler_params=_params(("parallel",)),
        name="merge",
    )(br_a, br_b, br_c, z, z, z, w_branch, w_out, x2, gate1)


def _ffn_kernel(x_ref, g_ref, sc_ref, sh_ref, gate_ref, wg_ref, wu_ref, wo_ref, o_ref,
                h_scr, r_scr):
    j = pl.program_id(1)

    @pl.when(j == 0)
    def _():
        _norm_modulate_into(x_ref, g_ref, sc_ref, sh_ref, h_scr, r_scr)
        o_ref[...] = jnp.zeros_like(o_ref)

    for r0 in range(0, TM_FFN, TM_FFN_SUB):
        h = h_scr[r0:r0 + TM_FFN_SUB, :]
        gp = _dot(h, wg_ref[...])
        up = _dot(h, wu_ref[...])
        act = ((gp * jax.nn.sigmoid(gp)) * up).astype(BF16)
        o_ref[r0:r0 + TM_FFN_SUB, :] += _dot(act, wo_ref[...])

    @pl.when(j == pl.num_programs(1) - 1)
    def _():
        o_ref[...] = x_ref[...] + gate_ref[0] * o_ref[...]


def _ffn(x2, g, scale, shift, gate, w_in, w_out, layer, seq):
    t, d = x2.shape
    per_b = seq // TM_FFN
    nj = FFN_DIM // TF_FFN
    vec = pl.BlockSpec((1, 1, d), lambda i, j: (i // per_b, 0, 0))
    return pl.pallas_call(
        _ffn_kernel,
        grid=(t // TM_FFN, nj),
        in_specs=[
            pl.BlockSpec((TM_FFN, d), lambda i, j: (i, 0)),
            pl.BlockSpec((1, d), lambda i, j: (0, 0)),
            vec, vec, vec,
            pl.BlockSpec((None, d, TF_FFN), lambda i, j: (layer, 0, j)),
            pl.BlockSpec((None, d, TF_FFN), lambda i, j: (layer, 0, nj + j)),
            pl.BlockSpec((None, TF_FFN, d), lambda i, j: (layer, j, 0)),
        ],
        out_specs=pl.BlockSpec((TM_FFN, d), lambda i, j: (i, 0)),
        out_shape=jax.ShapeDtypeStruct((t, d), F32),
        scratch_shapes=[pltpu.VMEM((TM_FFN, d), BF16), pltpu.VMEM((TM_FFN, LANES), F32)],
        compiler_params=_params(("parallel", "arbitrary")),
        name="ffn",
    )(x2, g.reshape(1, d), scale, shift, gate, w_in, w_in, w_out)


def _rope_lanes(a, axis):
    x1, x2 = jnp.split(a, 2, axis=axis)
    zero = jnp.zeros_like(x1)
    return jnp.concatenate([x1, zero, x2, zero], axis=axis)


def _in_offsets():
    offs, off = [], 0
    for size in IN_SIZES:
        offs.append(off)
        off += size
    return offs


(IN_QM, IN_KM, IN_VM, IN_OM, IN_IM, IN_FM, IN_QLAT, IN_KVLAT, IN_KR, IN_POOL, IN_GATES) = _in_offsets()
IN_W = sum(IN_SIZES)

_Z_COPIES = ((Z_GATES, IN_GATES, N_BRANCH * D_MODEL), (Z_QM, IN_QM, M_QK_W), (Z_QLAT, IN_QLAT, Q_RANK),
             (Z_VM, IN_VM, M_V_W), (Z_OM, IN_OM, M_V_W), (Z_KVLAT, IN_KVLAT, KV_RANK),
             (Z_POOL, IN_POOL, POOL_W))
PREP_ROWS = 128
PREP_COLS = 512


def _w_prep_kernel(w_ref, o_ref):
    rows = w_ref.shape[0]
    for dst, src, width in _Z_COPIES:
        for c in range(0, width, PREP_COLS):
            n = min(PREP_COLS, width - c)
            o_ref[:, dst + c:dst + c + n] = w_ref[:, src + c:src + c + n].astype(BF16)
    z32 = jnp.zeros((rows, ROPE_HALF), F32)
    kr = jnp.concatenate(
        [w_ref[:, IN_KR:IN_KR + ROPE_HALF], z32, w_ref[:, IN_KR + ROPE_HALF:IN_KR + ROPE_DIM], z32,
         jnp.zeros((rows, LANES), F32)], axis=1)
    o_ref[:, Z_KR:Z_KR + 2 * LANES] = kr.astype(BF16)


def _prep_w_z(w_in):
    depth, d, _ = w_in.shape
    return pl.pallas_call(
        _w_prep_kernel,
        grid=(depth, d // PREP_ROWS),
        in_specs=[pl.BlockSpec((None, PREP_ROWS, IN_W), lambda l, r: (l, r, 0))],
        out_specs=pl.BlockSpec((None, PREP_ROWS, Z_W), lambda l, r: (l, r, 0)),
        out_shape=jax.ShapeDtypeStruct((depth, d, Z_W), BF16),
        compiler_params=_params(("parallel", "parallel")),
        name="w_prep",
    )(w_in)


def _prep_w_gates(w):
    d = w.shape[0]
    w_kt = w[:, IN_KM:IN_KM + M_QK_W].T.astype(BF16)
    w_if = jnp.concatenate(
        [w[:, IN_IM:IN_IM + 2 * M_HEADS], jnp.zeros((d, GATE_ROWS - 2 * M_HEADS), w.dtype)],
        axis=1).T.astype(BF16)
    return w_kt, w_if


def _prep_head_cols(a):
    return jnp.concatenate([a[..., :NOPE_DIM], _rope_lanes(a[..., NOPE_DIM:], -1)], axis=-1)


def _rope_tables(seq):
    pos = jnp.arange(seq, dtype=F32)
    freqs = ROPE_THETA ** (-jnp.arange(0, ROPE_DIM, 2, dtype=F32) / ROPE_DIM)
    ang = pos[:, None] * freqs[None, :]
    cos, sin = jnp.cos(ang), jnp.sin(ang)
    zero = jnp.zeros_like(cos)
    cos_t = jnp.concatenate([cos, zero, cos, zero], axis=1)
    sin_t = jnp.concatenate([-sin, zero, sin, zero], axis=1)
    return cos_t, sin_t


def kernel(x, c, w_ada, b_ada, g_norm1, w_in, b_mgate, g_mnorm, g_qlat, w_uq, g_kvlat, w_ukv,
           g_qn, g_kn, w_pool, b_pool, s_pool, w_branch, w_out, g_norm2, w_ffn_in, w_ffn_out):
    batch, seq, d = x.shape
    depth = w_in.shape[0]
    t = batch * seq

    mod = _ada(c, w_ada, b_ada)
    cos_t, sin_t = _rope_tables(seq)
    x2 = x.reshape(t, d)
    w_pool_b, w_branch_b, w_out_b = w_pool.astype(BF16), w_branch.astype(BF16), w_out.astype(BF16)
    w_ffn_in_b, w_ffn_out_b = w_ffn_in.astype(BF16), w_ffn_out.astype(BF16)
    w_z = _prep_w_z(w_in)

    for l in range(depth):
        shift1, scale1, gate1, shift2, scale2, gate2 = [
            mod[l, :, k * d:(k + 1) * d].reshape(batch, 1, d) for k in range(6)]

        w_kt, w_if = _prep_w_gates(w_in[l])
        z, kt, gates_t = _inproj(x2, g_norm1[l], scale1, shift1, w_z, l, w_kt, w_if, seq)

        br_a = _mlstm(z.reshape(batch, seq, Z_W), kt, gates_t, b_mgate[l],
                      g_mnorm[l]).reshape(t, M_V_W)

        w_q = _prep_head_cols(w_uq[l].reshape(Q_RANK, MLA_HEADS, QK_DIM)).reshape(
            Q_RANK, MLA_HEADS * QK_PAD).astype(BF16)
        w_kv = w_ukv[l].reshape(KV_RANK, MLA_HEADS, NOPE_DIM + MLA_V)
        w_k = w_kv[..., :NOPE_DIM].reshape(KV_RANK, MLA_HEADS * NOPE_DIM).astype(BF16)
        w_v = w_kv[..., NOPE_DIM:].reshape(KV_RANK, MLA_W).astype(BF16)
        q, k, v = _mla_proj(z, g_qlat[l], g_kvlat[l], w_q, w_k, w_v,
                            _prep_head_cols(g_qn[l]).reshape(1, QK_PAD),
                            _prep_head_cols(g_kn[l]).reshape(1, QK_PAD), cos_t, sin_t, seq)
        br_b = _attention(q, k, v, batch, seq)

        br_c = _pool(z, w_pool_b, l, b_pool[l], s_pool[l], batch, seq)

        x2 = _merge(br_a, br_b, br_c, z, w_branch_b, w_out_b, l, x2, gate1, seq)
        x2 = _ffn(x2, g_norm2[l], scale2, shift2, gate2, w_ffn_in_b, w_ffn_out_b, l, seq)

    return x2.reshape(batch, seq, d)
```

```python
import functools

import jax
import jax.numpy as jnp
from jax import lax
from jax.experimental import pallas as pl
from jax.experimental.pallas import tpu as pltpu

F32 = jnp.float32
BF16 = jnp.bfloat16

D_MODEL = 2048
M_HEADS = 4
M_QK = 128
M_V = 256
M_QK_W = M_HEADS * M_QK
M_V_W = M_HEADS * M_V
M_CHUNK = 128
MLA_HEADS = 8
NOPE_DIM = 128
ROPE_DIM = 64
ROPE_HALF = ROPE_DIM // 2
QK_DIM = NOPE_DIM + ROPE_DIM
MLA_V = 128
MLA_W = MLA_HEADS * MLA_V
Q_RANK = 512
KV_RANK = 256
ROPE_THETA = 10000.0
POOL_WINDOWS = (2, 4, 8, 16)
POOL_GROUPS = 4
POOL_GROUP_W = 256
POOL_W = POOL_GROUPS * POOL_GROUP_W
N_BRANCH = 3
BRANCH_W = 1024
FFN_DIM = ((8 * D_MODEL // 3 + 255) // 256) * 256
NORM_EPS = 1e-6
IN_SIZES = (M_QK_W, M_QK_W, M_V_W, M_V_W, M_HEADS, M_HEADS, Q_RANK, KV_RANK, ROPE_DIM, POOL_W,
            N_BRANCH * D_MODEL)

LANES = 128
QK_PAD = 2 * LANES
GATE_ROWS = 16

Z_GATES = 0
Z_QM = Z_GATES + N_BRANCH * D_MODEL
Z_QLAT = Z_QM + M_QK_W
Z_VM = Z_QLAT + Q_RANK
Z_OM = Z_VM + M_V_W
Z_KVLAT = Z_OM + M_V_W
Z_KR = Z_KVLAT + KV_RANK
Z_POOL = Z_KR + 2 * LANES
Z_W = Z_POOL + POOL_W

V7X_VMEM_LIMIT = 52 * 1024 * 1024

TM_IN = 1024
TN_IN = 1536
NORM_ROWS = 64
NORM_UNROLL = 4
TM_MLA = 512
TQ = 256
TM_MERGE = 256
TM_FFN = 1024
TM_FFN_SUB = 512
TF_FFN = 512
TN_ADA = 1024


def _params(semantics):
    return pltpu.CompilerParams(dimension_semantics=semantics, vmem_limit_bytes=V7X_VMEM_LIMIT)


def _rms(x, width=None):
    ss = jnp.sum(x * x, axis=-1, keepdims=True)
    n = x.shape[-1] if width is None else width
    return x * lax.rsqrt(ss * (1.0 / n) + NORM_EPS)


def _dot(a, b):
    return jnp.dot(a, b, preferred_element_type=F32)


def _dot_nt(a, b):
    return lax.dot_general(a, b, (((1,), (1,)), ((), ())), preferred_element_type=F32)


def _ada_kernel(c_ref, w_ref, b_ref, o_ref):
    c = c_ref[...]
    ca = (c * jax.nn.sigmoid(c)).astype(BF16)
    o_ref[0] = _dot(ca, w_ref[0].astype(BF16)) + b_ref[0]


def _ada(c, w_ada, b_ada):
    depth, d, n = w_ada.shape
    b = c.shape[0]
    return pl.pallas_call(
        _ada_kernel,
        grid=(depth, n // TN_ADA),
        in_specs=[
            pl.BlockSpec((b, d), lambda l, j: (0, 0)),
            pl.BlockSpec((1, d, TN_ADA), lambda l, j: (l, 0, j)),
            pl.BlockSpec((1, 1, TN_ADA), lambda l, j: (l, 0, j)),
        ],
        out_specs=pl.BlockSpec((1, b, TN_ADA), lambda l, j: (l, 0, j)),
        out_shape=jax.ShapeDtypeStruct((depth, b, n), F32),
        compiler_params=_params(("parallel", "parallel")),
        name="adaln",
    )(c, w_ada, b_ada.reshape(depth, 1, n))


def _norm_modulate_into(x_ref, g_ref, sc_ref, sh_ref, h_scr, r_scr):
    n_slabs = x_ref.shape[0] // NORM_ROWS
    d = x_ref.shape[1]

    def stats(r, carry):
        r0 = pl.multiple_of(r * NORM_ROWS, NORM_ROWS)
        x = x_ref[pl.ds(r0, NORM_ROWS), :]
        ms = jnp.sum(x * x, axis=-1, keepdims=True) * (1.0 / d)
        r_scr[pl.ds(r0, NORM_ROWS), :] = jnp.broadcast_to(lax.rsqrt(ms + NORM_EPS), (NORM_ROWS, LANES))
        return carry

    lax.fori_loop(0, n_slabs, stats, 0, unroll=NORM_UNROLL)

    gain = g_ref[...] * (1.0 + sc_ref[0])
    shift = sh_ref[0]

    def apply(r, carry):
        r0 = pl.multiple_of(r * NORM_ROWS, NORM_ROWS)
        rstd = r_scr[pl.ds(r0, NORM_ROWS), :]
        for t in range(d // LANES):
            c0, c1 = t * LANES, (t + 1) * LANES
            x = x_ref[pl.ds(r0, NORM_ROWS), c0:c1]
            h_scr[pl.ds(r0, NORM_ROWS), c0:c1] = (x * rstd * gain[:, c0:c1] + shift[:, c0:c1]).astype(BF16)
        return carry

    lax.fori_loop(0, n_slabs, apply, 0)


def _inproj_kernel(x_ref, g_ref, sc_ref, sh_ref, w_ref, wkt_ref, wif_ref, z_ref, kt_ref, gt_ref,
                   h_scr, r_scr):
    @pl.when(pl.program_id(1) == 0)
    def _():
        _norm_modulate_into(x_ref, g_ref, sc_ref, sh_ref, h_scr, r_scr)
        h = h_scr[...]
        kt_ref[...] = _dot_nt(wkt_ref[...], h).astype(BF16)
        gt_ref[...] = _dot_nt(wif_ref[...], h)

    z_ref[...] = _dot_nt(h_scr[...], w_ref[...]).astype(BF16)


def _inproj(x2, g, scale, shift, w_z, layer, w_kt, w_if, seq):
    t, d = x2.shape
    per_b = seq // TM_IN
    vec = pl.BlockSpec((1, 1, d), lambda i, j: (i // per_b, 0, 0))
    return pl.pallas_call(
        _inproj_kernel,
        grid=(t // TM_IN, Z_W // TN_IN),
        in_specs=[
            pl.BlockSpec((TM_IN, d), lambda i, j: (i, 0)),
            pl.BlockSpec((1, d), lambda i, j: (0, 0)),
            vec, vec,
            pl.BlockSpec((None, TN_IN, d), lambda i, j: (layer, j, 0)),
            pl.BlockSpec((None, M_QK_W, d), lambda i, j: (layer, 0, 0)),
            pl.BlockSpec((None, GATE_ROWS, d), lambda i, j: (layer, 0, 0)),
        ],
        out_specs=[
            pl.BlockSpec((TM_IN, TN_IN), lambda i, j: (i, j)),
            pl.BlockSpec((M_QK_W, TM_IN), lambda i, j: (0, i)),
            pl.BlockSpec((GATE_ROWS, TM_IN), lambda i, j: (0, i)),
        ],
        out_shape=[
            jax.ShapeDtypeStruct((t, Z_W), BF16),
            jax.ShapeDtypeStruct((M_QK_W, t), BF16),
            jax.ShapeDtypeStruct((GATE_ROWS, t), F32),
        ],
        scratch_shapes=[pltpu.VMEM((TM_IN, d), BF16), pltpu.VMEM((TM_IN, LANES), F32)],
        compiler_params=_params(("parallel", "arbitrary")),
        name="inproj",
    )(x2, g.reshape(1, d), scale, shift, w_z, w_kt, w_if)


def _mlstm_kernel(bias_ref, q_ref, kt_ref, v_ref, o_ref, g_ref, gn_ref, out_ref,
                  b_scr, i_scr, ct_scr, nr_scr, num_scr, dn_scr):
    L = M_CHUNK
    nc = g_ref.shape[2]
    scale = M_QK ** -0.5

    lane = lax.broadcasted_iota(jnp.int32, (nc, L), 1)
    for h in range(M_HEADS):
        fg = g_ref[M_HEADS + h, 0] + bias_ref[1, h]
        bc = jnp.minimum(fg, 0.0) - jnp.log(1.0 + jnp.exp(-jnp.abs(fg)))
        sh = 1
        while sh < L:
            bc = bc + jnp.where(lane >= sh, pltpu.roll(bc, sh, axis=1), 0.0)
            sh *= 2
        b_scr[h] = bc
        i_scr[h] = g_ref[h, 0] + bias_ref[0, h]
    ct_scr[...] = jnp.zeros_like(ct_scr)
    nr_scr[...] = jnp.zeros_like(nr_scr)

    rows = lax.broadcasted_iota(jnp.int32, (L, L), 0)
    cols = lax.broadcasted_iota(jnp.int32, (L, L), 1)
    eye = rows == cols
    tril = rows >= cols
    ones = jnp.ones((L, LANES), BF16)

    def to_col(row):
        return jnp.sum(jnp.where(eye, jnp.broadcast_to(row, (L, L)), 0.0), axis=1, keepdims=True)

    def step(c, ms):
        r0 = c * L if isinstance(c, int) else pl.multiple_of(c * L, L)
        heads = range(M_HEADS)
        qb = [q_ref[0, pl.ds(r0, L), h * M_QK:(h + 1) * M_QK] for h in heads]
        kt = [kt_ref[h * M_QK:(h + 1) * M_QK, pl.ds(r0, L)] for h in heads]
        vb = [v_ref[0, pl.ds(r0, L), h * M_V:(h + 1) * M_V] for h in heads]
        b_row = [b_scr[h, pl.ds(c, 1), :] for h in heads]
        i_row = [i_scr[h, pl.ds(c, 1), :] for h in heads]
        ct = [ct_scr[h] for h in heads]
        nr = [nr_scr[h] for h in heads]

        qk = [_dot(qb[h], kt[h]) for h in heads]
        inter = [_dot(qb[h], ct[h].astype(BF16)) for h in heads]
        qn = [_dot(qb[h], nr[h].astype(BF16)) for h in heads]

        m_out = []
        for h in heads:
            b_last = b_row[h][:, L - 1:L]
            g_row = b_last - b_row[h] + i_row[h]
            m_new = jnp.maximum(b_last + ms[h], jnp.max(g_row, axis=1, keepdims=True))
            decay = jnp.exp(b_last + ms[h] - m_new)
            ktw = (kt[h].astype(F32) * (jnp.exp(g_row - m_new) * scale)).astype(BF16)
            ct_scr[h] = decay * ct[h] + _dot(ktw, vb[h])
            nr_scr[h] = decay * nr[h] + _dot(ktw, ones)
            m_out.append(m_new)

        w_inter, e_neg, sb = [], [], []
        for h in heads:
            b_col = to_col(b_row[h])
            a_col = b_col + ms[h]
            dm = jnp.where(tril, b_col - b_row[h] + i_row[h], -jnp.inf)
            m_t = jnp.maximum(a_col, jnp.max(dm, axis=1, keepdims=True))
            w_inter.append(jnp.exp(a_col - m_t))
            e_neg.append(jnp.exp(-m_t))
            sb.append(((qk[h] * scale) * jnp.exp(dm - m_t)).astype(BF16))

        sv = [_dot(sb[h], vb[h]) for h in heads]
        ssum = [_dot(sb[h], ones) for h in heads]

        for h in heads:
            num_scr[h] = w_inter[h] * inter[h] + sv[h]
            den = w_inter[h] * qn[h] + ssum[h]
            dn_scr[h] = jnp.maximum(jnp.abs(den), e_neg[h])
        return tuple(m_out)

    def finish(c):
        r0 = c * L if isinstance(c, int) else pl.multiple_of(c * L, L)
        for h in range(M_HEADS):
            inv = 1.0 / dn_scr[h]
            hh = num_scr[h] * jnp.concatenate([inv] * (M_V // LANES), axis=1)
            o = o_ref[0, pl.ds(r0, L), h * M_V:(h + 1) * M_V].astype(F32)
            out_ref[0, pl.ds(r0, L), h * M_V:(h + 1) * M_V] = (
                (_rms(hh) * gn_ref[h:h + 1, :]) * jax.nn.sigmoid(o)).astype(BF16)

    def pipelined(c, ms):
        finish(c - 1)
        return step(c, ms)

    ms = step(0, tuple(jnp.zeros((1, 1), F32) for _ in range(M_HEADS)))
    lax.fori_loop(1, nc, pipelined, ms)
    finish(nc - 1)


def _mlstm(z3, kt, gates_t, b_mgate, g_mnorm):
    b, s, _ = z3.shape
    nc = s // M_CHUNK
    g4 = gates_t.reshape(GATE_ROWS, b, nc, M_CHUNK)
    return pl.pallas_call(
        _mlstm_kernel,
        grid=(b,),
        in_specs=[
            pl.BlockSpec(memory_space=pltpu.SMEM),
            pl.BlockSpec((1, s, M_QK_W), lambda i: (i, 0, Z_QM // M_QK_W)),
            pl.BlockSpec((M_QK_W, s), lambda i: (0, i)),
            pl.BlockSpec((1, s, M_V_W), lambda i: (i, 0, Z_VM // M_V_W)),
            pl.BlockSpec((1, s, M_V_W), lambda i: (i, 0, Z_OM // M_V_W)),
            pl.BlockSpec((GATE_ROWS, 1, nc, M_CHUNK), lambda i: (0, i, 0, 0)),
            pl.BlockSpec((M_HEADS, M_V), lambda i: (0, 0)),
        ],
        out_specs=pl.BlockSpec((1, s, M_V_W), lambda i: (i, 0, 0)),
        out_shape=jax.ShapeDtypeStruct((b, s, M_V_W), BF16),
        scratch_shapes=[
            pltpu.VMEM((M_HEADS, nc, M_CHUNK), F32), pltpu.VMEM((M_HEADS, nc, M_CHUNK), F32),
            pltpu.VMEM((M_HEADS, M_QK, M_V), F32), pltpu.VMEM((M_HEADS, M_QK, LANES), F32),
            pltpu.VMEM((M_HEADS, M_CHUNK, M_V), F32), pltpu.VMEM((M_HEADS, M_CHUNK, LANES), F32),
        ],
        compiler_params=_params(("parallel",)),
        name="mlstm",
    )(b_mgate, z3, kt, z3, z3, g4, g_mnorm)


def _mla_proj_kernel(ql_ref, kvl_ref, kr_ref, krs_ref, gq_ref, gkv_ref, wq_ref, wqs_ref, wk_ref,
                     wv_ref, gqn_ref, gqs_ref, gkn_ref, gks_ref, cos_ref, sin_ref,
                     q_out, k_out, v_out):
    scale = QK_DIM ** -0.5
    qn = (_rms(ql_ref[...].astype(F32)) * gq_ref[...]).astype(BF16)
    kvn = (_rms(kvl_ref[...].astype(F32)) * gkv_ref[...]).astype(BF16)
    qf = _dot(qn, wq_ref[...])
    qs = _dot(qn, wqs_ref[...])
    kf = _dot(kvn, wk_ref[...])
    v_out[...] = _dot(kvn, wv_ref[...]).astype(BF16)

    cos_t = cos_ref[...]
    sin_t = sin_ref[...]
    gqn = gqn_ref[...] * scale
    gkn = gkn_ref[...]
    q_cos = gqn[:, LANES:] * cos_t
    q_sin = (gqs_ref[...] * scale) * sin_t
    kr = kr_ref[...].astype(F32)
    kr_ss = jnp.sum(kr * kr, axis=-1, keepdims=True)
    k_rot = kr * (gkn[:, LANES:] * cos_t) + krs_ref[...].astype(F32) * (gks_ref[...] * sin_t)
    for h in range(MLA_HEADS):
        qh = qf[:, h * QK_PAD:(h + 1) * QK_PAD]
        rq = lax.rsqrt(jnp.sum(qh * qh, axis=-1, keepdims=True) * (1.0 / QK_DIM) + NORM_EPS)
        q_out[:, h * QK_PAD:h * QK_PAD + LANES] = (qh[:, :LANES] * rq * gqn[:, :LANES]).astype(BF16)
        q_out[:, h * QK_PAD + LANES:(h + 1) * QK_PAD] = (
            (qh[:, LANES:] * q_cos + qs[:, h * LANES:(h + 1) * LANES] * q_sin) * rq).astype(BF16)

        kh = kf[:, h * NOPE_DIM:(h + 1) * NOPE_DIM]
        ss = jnp.sum(kh * kh, axis=-1, keepdims=True) + kr_ss
        rk = lax.rsqrt(ss * (1.0 / QK_DIM) + NORM_EPS)
        k_out[:, h * QK_PAD:h * QK_PAD + LANES] = (kh * rk * gkn[:, :LANES]).astype(BF16)
        k_out[:, h * QK_PAD + LANES:(h + 1) * QK_PAD] = (k_rot * rk).astype(BF16)


def _mla_proj(z, g_qlat, g_kvlat, w_q, w_qs, w_k, w_v, g_qn, g_qs, g_kn, g_ks, cos_t, sin_t, seq):
    t = z.shape[0]
    per_b = seq // TM_MLA
    full = lambda shape: pl.BlockSpec(shape, lambda i: (0,) * len(shape))
    return pl.pallas_call(
        _mla_proj_kernel,
        grid=(t // TM_MLA,),
        in_specs=[
            pl.BlockSpec((TM_MLA, Q_RANK), lambda i: (i, Z_QLAT // Q_RANK)),
            pl.BlockSpec((TM_MLA, KV_RANK), lambda i: (i, Z_KVLAT // KV_RANK)),
            pl.BlockSpec((TM_MLA, LANES), lambda i: (i, Z_KR // LANES)),
            pl.BlockSpec((TM_MLA, LANES), lambda i: (i, Z_KR // LANES + 1)),
            full((1, Q_RANK)), full((1, KV_RANK)),
            full(w_q.shape), full(w_qs.shape), full(w_k.shape), full(w_v.shape),
            full((1, QK_PAD)), full((1, LANES)), full((1, QK_PAD)), full((1, LANES)),
            pl.BlockSpec((TM_MLA, LANES), lambda i: (i % per_b, 0)),
            pl.BlockSpec((TM_MLA, LANES), lambda i: (i % per_b, 0)),
        ],
        out_specs=[
            pl.BlockSpec((TM_MLA, MLA_HEADS * QK_PAD), lambda i: (i, 0)),
            pl.BlockSpec((TM_MLA, MLA_HEADS * QK_PAD), lambda i: (i, 0)),
            pl.BlockSpec((TM_MLA, MLA_W), lambda i: (i, 0)),
        ],
        out_shape=[
            jax.ShapeDtypeStruct((t, MLA_HEADS * QK_PAD), BF16),
            jax.ShapeDtypeStruct((t, MLA_HEADS * QK_PAD), BF16),
            jax.ShapeDtypeStruct((t, MLA_W), BF16),
        ],
        compiler_params=_params(("parallel",)),
        name="mla_proj",
    )(z, z, z, z, g_qlat.reshape(1, Q_RANK), g_kvlat.reshape(1, KV_RANK), w_q, w_qs, w_k, w_v,
      g_qn, g_qs, g_kn, g_ks, cos_t, sin_t)


def _attn_kernel(q_ref, k_ref, v_ref, o_ref):
    s = q_ref.shape[0]
    rows = lax.broadcasted_iota(jnp.int32, (TQ, TQ), 0)
    cols = lax.broadcasted_iota(jnp.int32, (TQ, TQ), 1)
    causal = rows >= cols

    def logits(blk):
        s0, s1 = blk * TQ, (blk + 1) * TQ
        q = q_ref[s0:s1, :]
        diag = _dot_nt(q, k_ref[s0:s1, :])
        past = _dot_nt(q, k_ref[:s0, :]) if blk else None
        return diag, past

    n_blk = s // TQ
    nxt = logits(0)
    for blk in range(n_blk):
        s0, s1 = blk * TQ, (blk + 1) * TQ
        diag, past = nxt
        if blk + 1 < n_blk:
            nxt = logits(blk + 1)
        diag = jnp.where(causal, diag, -jnp.inf)
        m = jnp.max(diag, axis=-1, keepdims=True)
        if blk:
            m = jnp.maximum(m, jnp.max(past, axis=-1, keepdims=True))
        p = jnp.exp(diag - m)
        denom = jnp.sum(p, axis=-1, keepdims=True)
        acc = _dot(p.astype(BF16), v_ref[s0:s1, :])
        if blk:
            pp = jnp.exp(past - m)
            denom = denom + jnp.sum(pp, axis=-1, keepdims=True)
            acc = acc + _dot(pp.astype(BF16), v_ref[:s0, :])
        o_ref[s0:s1, :] = (acc * (1.0 / denom)).astype(BF16)


def _attention(q, k, v, batch, seq):
    t = q.shape[0]
    return pl.pallas_call(
        _attn_kernel,
        grid=(batch, MLA_HEADS),
        in_specs=[
            pl.BlockSpec((seq, QK_PAD), lambda b, h: (b, h)),
            pl.BlockSpec((seq, QK_PAD), lambda b, h: (b, h)),
            pl.BlockSpec((seq, MLA_V), lambda b, h: (b, h)),
        ],
        out_specs=pl.BlockSpec((seq, MLA_V), lambda b, h: (b, h)),
        out_shape=jax.ShapeDtypeStruct((t, MLA_W), BF16),
        compiler_params=_params(("parallel", "parallel")),
        name="attention",
    )(q, k, v)


def _pool_kernel(u0_ref, u1_ref, u2_ref, u3_ref, w_ref, b_ref, s_ref, o_ref):
    s = o_ref.shape[0]
    row = lax.broadcasted_iota(jnp.int32, (s, 1), 0)
    for g, (u_ref, win) in enumerate(zip((u0_ref, u1_ref, u2_ref, u3_ref), POOL_WINDOWS)):
        c0, c1 = g * POOL_GROUP_W, (g + 1) * POOL_GROUP_W
        u = u_ref[...].astype(F32)
        acc = u
        sh = 1
        while sh < win:
            acc = acc + jnp.where(row >= sh, pltpu.roll(acc, sh, axis=0), 0.0)
            sh *= 2
        inv_cnt = 1.0 / jnp.minimum(row + 1, win).astype(F32)
        pooled = (acc * inv_cnt - u).astype(BF16)
        y = _dot(pooled, w_ref[g]) + b_ref[:, c0:c1]
        o_ref[:, c0:c1] = (y * s_ref[:, c0:c1]).astype(BF16)


def _pool(z, w_pool, layer, b_pool, s_pool, batch, seq):
    t = z.shape[0]
    group = lambda g: pl.BlockSpec((seq, POOL_GROUP_W), lambda b: (b, Z_POOL // POOL_GROUP_W + g))
    return pl.pallas_call(
        _pool_kernel,
        grid=(batch,),
        in_specs=[
            group(0), group(1), group(2), group(3),
            pl.BlockSpec((None,) + w_pool.shape[1:], lambda b: (layer, 0, 0, 0)),
            pl.BlockSpec((1, POOL_W), lambda b: (0, 0)),
            pl.BlockSpec((1, POOL_W), lambda b: (0, 0)),
        ],
        out_specs=pl.BlockSpec((seq, POOL_W), lambda b: (b, 0)),
        out_shape=jax.ShapeDtypeStruct((t, POOL_W), BF16),
        compiler_params=_params(("parallel",)),
        name="pool",
    )(z, z, z, z, w_pool, b_pool.reshape(1, POOL_W), s_pool.reshape(1, POOL_W))


def _merge_kernel(a_ref, b_ref, c_ref, ga_ref, gb_ref, gc_ref, wb_ref, wo_ref, x_ref, gate_ref,
                  o_ref):
    merged = (jax.nn.sigmoid(ga_ref[...].astype(F32)) * _dot(a_ref[...], wb_ref[0])
              + jax.nn.sigmoid(gb_ref[...].astype(F32)) * _dot(b_ref[...], wb_ref[1])
              + jax.nn.sigmoid(gc_ref[...].astype(F32)) * _dot(c_ref[...], wb_ref[2]))
    o_ref[...] = x_ref[...] + gate_ref[0] * _dot(merged.astype(BF16), wo_ref[...])


def _merge(br_a, br_b, br_c, z, w_branch, w_out, layer, x2, gate1, seq):
    t, d = x2.shape
    per_b = seq // TM_MERGE
    g0 = Z_GATES // d
    br = pl.BlockSpec((TM_MERGE, BRANCH_W), lambda i: (i, 0))
    gate = lambda k: pl.BlockSpec((TM_MERGE, d), lambda i: (i, g0 + k))
    return pl.pallas_call(
        _merge_kernel,
        grid=(t // TM_MERGE,),
        in_specs=[
            br, br, br, gate(0), gate(1), gate(2),
            pl.BlockSpec((None, N_BRANCH, BRANCH_W, d), lambda i: (layer, 0, 0, 0),
                         pipeline_mode=pl.Buffered(1)),
            pl.BlockSpec((None, d, d), lambda i: (layer, 0, 0), pipeline_mode=pl.Buffered(1)),
            pl.BlockSpec((TM_MERGE, d), lambda i: (i, 0)),
            pl.BlockSpec((1, 1, d), lambda i: (i // per_b, 0, 0)),
        ],
        out_specs=pl.BlockSpec((TM_MERGE, d), lambda i: (i, 0)),
        out_shape=jax.ShapeDtypeStruct((t, d), F32),
        compiler_params=_params(("parallel",)),
        name="merge",
    )(br_a, br_b, br_c, z, z, z, w_branch, w_out, x2, gate1)


def _ffn_kernel(x_ref, g_ref, sc_ref, sh_ref, gate_ref, wg_ref, wu_ref, wo_ref, o_ref,
                h_scr, r_scr):
    j = pl.program_id(1)

    @pl.when(j == 0)
    def _():
        _norm_modulate_into(x_ref, g_ref, sc_ref, sh_ref, h_scr, r_scr)
        o_ref[...] = jnp.zeros_like(o_ref)

    for r0 in range(0, TM_FFN, TM_FFN_SUB):
        h = h_scr[r0:r0 + TM_FFN_SUB, :]
        gp = _dot(h, wg_ref[...])
        up = _dot(h, wu_ref[...])
        act = ((gp * jax.nn.sigmoid(gp)) * up).astype(BF16)
        o_ref[r0:r0 + TM_FFN_SUB, :] += _dot(act, wo_ref[...])

    @pl.when(j == pl.num_programs(1) - 1)
    def _():
        o_ref[...] = x_ref[...] + gate_ref[0] * o_ref[...]


def _ffn(x2, g, scale, shift, gate, w_in, w_out, layer, seq):
    t, d = x2.shape
    per_b = seq // TM_FFN
    nj = FFN_DIM // TF_FFN
    vec = pl.BlockSpec((1, 1, d), lambda i, j: (i // per_b, 0, 0))
    return pl.pallas_call(
        _ffn_kernel,
        grid=(t // TM_FFN, nj),
        in_specs=[
            pl.BlockSpec((TM_FFN, d), lambda i, j: (i, 0)),
            pl.BlockSpec((1, d), lambda i, j: (0, 0)),
            vec, vec, vec,
            pl.BlockSpec((None, d, TF_FFN), lambda i, j: (layer, 0, j)),
            pl.BlockSpec((None, d, TF_FFN), lambda i, j: (layer, 0, nj + j)),
            pl.BlockSpec((None, TF_FFN, d), lambda i, j: (layer, j, 0)),
        ],
        out_specs=pl.BlockSpec((TM_FFN, d), lambda i, j: (i, 0)),
        out_shape=jax.ShapeDtypeStruct((t, d), F32),
        scratch_shapes=[pltpu.VMEM((TM_FFN, d), BF16), pltpu.VMEM((TM_FFN, LANES), F32)],
        compiler_params=_params(("parallel", "arbitrary")),
        name="ffn",
    )(x2, g.reshape(1, d), scale, shift, gate, w_in, w_in, w_out)


def _rope_lanes(a, axis):
    x1, x2 = jnp.split(a, 2, axis=axis)
    zero = jnp.zeros_like(x1)
    return jnp.concatenate([x1, zero, x2, zero], axis=axis)


def _in_offsets():
    offs, off = [], 0
    for size in IN_SIZES:
        offs.append(off)
        off += size
    return offs


(IN_QM, IN_KM, IN_VM, IN_OM, IN_IM, IN_FM, IN_QLAT, IN_KVLAT, IN_KR, IN_POOL, IN_GATES) = _in_offsets()
IN_W = sum(IN_SIZES)

_Z_COPIES = ((Z_GATES, IN_GATES, N_BRANCH * D_MODEL), (Z_QM, IN_QM, M_QK_W), (Z_QLAT, IN_QLAT, Q_RANK),
             (Z_VM, IN_VM, M_V_W), (Z_OM, IN_OM, M_V_W), (Z_KVLAT, IN_KVLAT, KV_RANK),
             (Z_POOL, IN_POOL, POOL_W))
PREP_COLS = 256
PREP_ROWS = 512


def _w_prep_kernel(w_ref, o_ref, kt_ref, if_ref):
    cols = w_ref.shape[1]
    kt_ref[...] = w_ref[IN_KM:IN_KM + M_QK_W, :].astype(BF16)
    if_ref[...] = jnp.concatenate(
        [w_ref[IN_IM:IN_IM + 2 * M_HEADS, :], jnp.zeros((GATE_ROWS - 2 * M_HEADS, cols), F32)],
        axis=0).astype(BF16)
    for dst, src, width in _Z_COPIES:
        for r in range(0, width, PREP_ROWS):
            n = min(PREP_ROWS, width - r)
            o_ref[dst + r:dst + r + n, :] = w_ref[src + r:src + r + n, :].astype(BF16)
    zero = jnp.zeros((ROPE_HALF, cols), BF16)
    x1 = w_ref[IN_KR:IN_KR + ROPE_HALF, :].astype(BF16)
    x2 = w_ref[IN_KR + ROPE_HALF:IN_KR + ROPE_DIM, :].astype(BF16)
    for k, part in enumerate((x1, zero, x2, zero, x2, zero, x1, zero)):
        o_ref[Z_KR + k * ROPE_HALF:Z_KR + (k + 1) * ROPE_HALF, :] = part


def _prep_w_in(w_in):
    depth, d, _ = w_in.shape
    return pl.pallas_call(
        _w_prep_kernel,
        grid=(depth, d // PREP_COLS),
        in_specs=[pl.BlockSpec((None, IN_W, PREP_COLS), lambda l, c: (l, 0, c))],
        out_specs=[
            pl.BlockSpec((None, Z_W, PREP_COLS), lambda l, c: (l, 0, c)),
            pl.BlockSpec((None, M_QK_W, PREP_COLS), lambda l, c: (l, 0, c)),
            pl.BlockSpec((None, GATE_ROWS, PREP_COLS), lambda l, c: (l, 0, c)),
        ],
        out_shape=[
            jax.ShapeDtypeStruct((depth, Z_W, d), BF16),
            jax.ShapeDtypeStruct((depth, M_QK_W, d), BF16),
            jax.ShapeDtypeStruct((depth, GATE_ROWS, d), BF16),
        ],
        compiler_params=_params(("parallel", "parallel")),
        name="w_prep",
    )(jnp.swapaxes(w_in, 1, 2))


def _prep_head_cols(a):
    return jnp.concatenate([a[..., :NOPE_DIM], _rope_lanes(a[..., NOPE_DIM:], -1)], axis=-1)


def _prep_rope_swapped(a):
    return _rope_lanes(
        jnp.concatenate([a[..., NOPE_DIM + ROPE_HALF:], a[..., NOPE_DIM:NOPE_DIM + ROPE_HALF]],
                        axis=-1), -1)


def _rope_tables(seq):
    pos = jnp.arange(seq, dtype=F32)
    freqs = ROPE_THETA ** (-jnp.arange(0, ROPE_DIM, 2, dtype=F32) / ROPE_DIM)
    ang = pos[:, None] * freqs[None, :]
    cos, sin = jnp.cos(ang), jnp.sin(ang)
    zero = jnp.zeros_like(cos)
    cos_t = jnp.concatenate([cos, zero, cos, zero], axis=1)
    sin_t = jnp.concatenate([-sin, zero, sin, zero], axis=1)
    return cos_t, sin_t


def kernel(x, c, w_ada, b_ada, g_norm1, w_in, b_mgate, g_mnorm, g_qlat, w_uq, g_kvlat, w_ukv,
           g_qn, g_kn, w_pool, b_pool, s_pool, w_branch, w_out, g_norm2, w_ffn_in, w_ffn_out):
    batch, seq, d = x.shape
    depth = w_in.shape[0]
    t = batch * seq

    mod = _ada(c, w_ada, b_ada)
    cos_t, sin_t = _rope_tables(seq)
    x2 = x.reshape(t, d)
    w_pool_b, w_branch_b, w_out_b = w_pool.astype(BF16), w_branch.astype(BF16), w_out.astype(BF16)
    w_ffn_in_b, w_ffn_out_b = w_ffn_in.astype(BF16), w_ffn_out.astype(BF16)
    w_z, w_kt, w_if = _prep_w_in(w_in)

    for l in range(depth):
        shift1, scale1, gate1, shift2, scale2, gate2 = [
            mod[l, :, k * d:(k + 1) * d].reshape(batch, 1, d) for k in range(6)]

        z, kt, gates_t = _inproj(x2, g_norm1[l], scale1, shift1, w_z, l, w_kt, w_if, seq)

        br_a = _mlstm(z.reshape(batch, seq, Z_W), kt, gates_t, b_mgate[l],
                      g_mnorm[l]).reshape(t, M_V_W)

        w_uq_h = w_uq[l].reshape(Q_RANK, MLA_HEADS, QK_DIM)
        w_q = _prep_head_cols(w_uq_h).reshape(Q_RANK, MLA_HEADS * QK_PAD).astype(BF16)
        w_qs = _prep_rope_swapped(w_uq_h).reshape(Q_RANK, MLA_HEADS * LANES).astype(BF16)
        w_kv = w_ukv[l].reshape(KV_RANK, MLA_HEADS, NOPE_DIM + MLA_V)
        w_k = w_kv[..., :NOPE_DIM].reshape(KV_RANK, MLA_HEADS * NOPE_DIM).astype(BF16)
        w_v = w_kv[..., NOPE_DIM:].reshape(KV_RANK, MLA_W).astype(BF16)
        q, k, v = _mla_proj(z, g_qlat[l], g_kvlat[l], w_q, w_qs, w_k, w_v,
                            _prep_head_cols(g_qn[l]).reshape(1, QK_PAD),
                            _prep_rope_swapped(g_qn[l]).reshape(1, LANES),
                            _prep_head_cols(g_kn[l]).reshape(1, QK_PAD),
                            _prep_rope_swapped(g_kn[l]).reshape(1, LANES), cos_t, sin_t, seq)
        br_b = _attention(q, k, v, batch, seq)

        br_c = _pool(z, w_pool_b, l, b_pool[l], s_pool[l], batch, seq)

        x2 = _merge(br_a, br_b, br_c, z, w_branch_b, w_out_b, l, x2, gate1, seq)
        x2 = _ffn(x2, g_norm2[l], scale2, shift2, gate2, w_ffn_in_b, w_ffn_out_b, l, seq)

    return x2.reshape(batch, seq, d)
```

```python
import functools

import jax
import jax.numpy as jnp
from jax import lax
from jax.experimental import pallas as pl
from jax.experimental.pallas import tpu as pltpu

F32 = jnp.float32
BF16 = jnp.bfloat16

D_MODEL = 2048
M_HEADS = 4
M_QK = 128
M_V = 256
M_QK_W = M_HEADS * M_QK
M_V_W = M_HEADS * M_V
M_CHUNK = 128
MLA_HEADS = 8
NOPE_DIM = 128
ROPE_DIM = 64
ROPE_HALF = ROPE_DIM // 2
QK_DIM = NOPE_DIM + ROPE_DIM
MLA_V = 128
MLA_W = MLA_HEADS * MLA_V
Q_RANK = 512
KV_RANK = 256
ROPE_THETA = 10000.0
POOL_WINDOWS = (2, 4, 8, 16)
POOL_GROUPS = 4
POOL_GROUP_W = 256
POOL_W = POOL_GROUPS * POOL_GROUP_W
N_BRANCH = 3
BRANCH_W = 1024
FFN_DIM = ((8 * D_MODEL // 3 + 255) // 256) * 256
NORM_EPS = 1e-6
LOG2_E = 1.4426950408889634
IN_SIZES = (M_QK_W, M_QK_W, M_V_W, M_V_W, M_HEADS, M_HEADS, Q_RANK, KV_RANK, ROPE_DIM, POOL_W,
            N_BRANCH * D_MODEL)

LANES = 128
QK_PAD = 2 * LANES
GATE_ROWS = 16

Z_GATES = 0
Z_QM = Z_GATES + N_BRANCH * D_MODEL
Z_QLAT = Z_QM + M_QK_W
Z_VM = Z_QLAT + Q_RANK
Z_OM = Z_VM + M_V_W
Z_KVLAT = Z_OM + M_V_W
Z_KR = Z_KVLAT + KV_RANK
Z_POOL = Z_KR + 2 * LANES
Z_W = Z_POOL + POOL_W

V7X_VMEM_LIMIT = 52 * 1024 * 1024

TM_IN = 1024
TN_IN = 1536
NORM_ROWS = 64
NORM_UNROLL = 4
TM_MLA = 1024
TQ = 256
TM_MERGE = 256
TM_FFN = 1024
TM_FFN_SUB = 512
TF_FFN = 512
TN_ADA = 1024


def _params(semantics):
    return pltpu.CompilerParams(dimension_semantics=semantics, vmem_limit_bytes=V7X_VMEM_LIMIT)


def _rms(x, width=None):
    ss = jnp.sum(x * x, axis=-1, keepdims=True)
    n = x.shape[-1] if width is None else width
    return x * lax.rsqrt(ss * (1.0 / n) + NORM_EPS)


def _dot(a, b):
    return jnp.dot(a, b, preferred_element_type=F32)


def _dot_nt(a, b):
    return lax.dot_general(a, b, (((1,), (1,)), ((), ())), preferred_element_type=F32)


def _ada_kernel(c_ref, w_ref, b_ref, o_ref):
    c = c_ref[...]
    ca = (c * jax.nn.sigmoid(c)).astype(BF16)
    o_ref[0] = _dot(ca, w_ref[0].astype(BF16)) + b_ref[0]


def _ada(c, w_ada, b_ada):
    depth, d, n = w_ada.shape
    b = c.shape[0]
    return pl.pallas_call(
        _ada_kernel,
        grid=(depth, n // TN_ADA),
        in_specs=[
            pl.BlockSpec((b, d), lambda l, j: (0, 0)),
            pl.BlockSpec((1, d, TN_ADA), lambda l, j: (l, 0, j)),
            pl.BlockSpec((1, 1, TN_ADA), lambda l, j: (l, 0, j)),
        ],
        out_specs=pl.BlockSpec((1, b, TN_ADA), lambda l, j: (l, 0, j)),
        out_shape=jax.ShapeDtypeStruct((depth, b, n), F32),
        compiler_params=_params(("parallel", "parallel")),
        name="adaln",
    )(c, w_ada, b_ada.reshape(depth, 1, n))


def _norm_modulate_into(x_ref, g_ref, sc_ref, sh_ref, h_scr, r_scr):
    n_slabs = x_ref.shape[0] // NORM_ROWS
    d = x_ref.shape[1]

    def stats(r, carry):
        r0 = pl.multiple_of(r * NORM_ROWS, NORM_ROWS)
        x = x_ref[pl.ds(r0, NORM_ROWS), :]
        ms = jnp.sum(x * x, axis=-1, keepdims=True) * (1.0 / d)
        r_scr[pl.ds(r0, NORM_ROWS), :] = jnp.broadcast_to(lax.rsqrt(ms + NORM_EPS), (NORM_ROWS, LANES))
        return carry

    lax.fori_loop(0, n_slabs, stats, 0, unroll=NORM_UNROLL)

    gain = g_ref[...] * (1.0 + sc_ref[0])
    shift = sh_ref[0]

    def apply(r, carry):
        r0 = pl.multiple_of(r * NORM_ROWS, NORM_ROWS)
        rstd = r_scr[pl.ds(r0, NORM_ROWS), :]
        for t in range(d // LANES):
            c0, c1 = t * LANES, (t + 1) * LANES
            x = x_ref[pl.ds(r0, NORM_ROWS), c0:c1]
            h_scr[pl.ds(r0, NORM_ROWS), c0:c1] = (x * rstd * gain[:, c0:c1] + shift[:, c0:c1]).astype(BF16)
        return carry

    lax.fori_loop(0, n_slabs, apply, 0)


def _inproj_kernel(x_ref, g_ref, sc_ref, sh_ref, w_ref, wkt_ref, wif_ref, z_ref, kt_ref, gt_ref,
                   h_scr, r_scr):
    @pl.when(pl.program_id(1) == 0)
    def _():
        _norm_modulate_into(x_ref, g_ref, sc_ref, sh_ref, h_scr, r_scr)
        h = h_scr[...]
        kt_ref[...] = _dot_nt(wkt_ref[...], h).astype(BF16)
        gt_ref[...] = _dot_nt(wif_ref[...], h)

    z_ref[...] = _dot_nt(h_scr[...], w_ref[...]).astype(BF16)


def _inproj(x2, g, scale, shift, w_z, layer, w_kt, w_if, seq):
    t, d = x2.shape
    per_b = seq // TM_IN
    vec = pl.BlockSpec((1, 1, d), lambda i, j: (i // per_b, 0, 0))
    return pl.pallas_call(
        _inproj_kernel,
        grid=(t // TM_IN, Z_W // TN_IN),
        in_specs=[
            pl.BlockSpec((TM_IN, d), lambda i, j: (i, 0)),
            pl.BlockSpec((1, d), lambda i, j: (0, 0)),
            vec, vec,
            pl.BlockSpec((None, TN_IN, d), lambda i, j: (layer, j, 0)),
            pl.BlockSpec((None, M_QK_W, d), lambda i, j: (layer, 0, 0)),
            pl.BlockSpec((None, GATE_ROWS, d), lambda i, j: (layer, 0, 0)),
        ],
        out_specs=[
            pl.BlockSpec((TM_IN, TN_IN), lambda i, j: (i, j)),
            pl.BlockSpec((M_QK_W, TM_IN), lambda i, j: (0, i)),
            pl.BlockSpec((GATE_ROWS, TM_IN), lambda i, j: (0, i)),
        ],
        out_shape=[
            jax.ShapeDtypeStruct((t, Z_W), BF16),
            jax.ShapeDtypeStruct((M_QK_W, t), BF16),
            jax.ShapeDtypeStruct((GATE_ROWS, t), F32),
        ],
        scratch_shapes=[pltpu.VMEM((TM_IN, d), BF16), pltpu.VMEM((TM_IN, LANES), F32)],
        compiler_params=_params(("parallel", "arbitrary")),
        name="inproj",
    )(x2, g.reshape(1, d), scale, shift, w_z, w_kt, w_if)


def _mlstm_kernel(bias_ref, q_ref, kt_ref, v_ref, o_ref, g_ref, gn_ref, out_ref,
                  b_scr, i_scr, ct_scr, nr_scr, num_scr, dn_scr):
    L = M_CHUNK
    nc = g_ref.shape[2]
    scale = M_QK ** -0.5

    lane = lax.broadcasted_iota(jnp.int32, (nc, L), 1)
    for h in range(M_HEADS):
        fg = g_ref[M_HEADS + h, 0] + bias_ref[1, h]
        bc = jnp.minimum(fg, 0.0) - jnp.log(1.0 + jnp.exp(-jnp.abs(fg)))
        sh = 1
        while sh < L:
            bc = bc + jnp.where(lane >= sh, pltpu.roll(bc, sh, axis=1), 0.0)
            sh *= 2
        b_scr[h] = bc
        i_scr[h] = g_ref[h, 0] + bias_ref[0, h]
    ct_scr[...] = jnp.zeros_like(ct_scr)
    nr_scr[...] = jnp.zeros_like(nr_scr)

    rows = lax.broadcasted_iota(jnp.int32, (L, L), 0)
    cols = lax.broadcasted_iota(jnp.int32, (L, L), 1)
    eye = rows == cols
    tril = rows >= cols
    ones = jnp.ones((L, LANES), BF16)

    def to_col(row):
        return jnp.sum(jnp.where(eye, jnp.broadcast_to(row, (L, L)), 0.0), axis=1, keepdims=True)

    def step(c, ms):
        r0 = c * L if isinstance(c, int) else pl.multiple_of(c * L, L)
        heads = range(M_HEADS)
        qb = [q_ref[0, pl.ds(r0, L), h * M_QK:(h + 1) * M_QK] for h in heads]
        kt = [kt_ref[h * M_QK:(h + 1) * M_QK, pl.ds(r0, L)] for h in heads]
        vb = [v_ref[0, pl.ds(r0, L), h * M_V:(h + 1) * M_V] for h in heads]
        b_row = [b_scr[h, pl.ds(c, 1), :] for h in heads]
        i_row = [i_scr[h, pl.ds(c, 1), :] for h in heads]
        ct = [ct_scr[h] for h in heads]
        nr = [nr_scr[h] for h in heads]

        qk = [_dot(qb[h], kt[h]) for h in heads]
        inter = [_dot(qb[h], ct[h].astype(BF16)) for h in heads]
        qn = [_dot(qb[h], nr[h].astype(BF16)) for h in heads]

        m_out = []
        for h in heads:
            b_last = b_row[h][:, L - 1:L]
            g_row = b_last - b_row[h] + i_row[h]
            m_new = jnp.maximum(b_last + ms[h], jnp.max(g_row, axis=1, keepdims=True))
            decay = jnp.exp(b_last + ms[h] - m_new)
            ktw = (kt[h].astype(F32) * (jnp.exp(g_row - m_new) * scale)).astype(BF16)
            ct_scr[h] = decay * ct[h] + _dot(ktw, vb[h])
            nr_scr[h] = decay * nr[h] + _dot(ktw, ones)
            m_out.append(m_new)

        w_inter, e_neg, sb = [], [], []
        for h in heads:
            b_col = to_col(b_row[h])
            a_col = b_col + ms[h]
            dm = jnp.where(tril, b_col - b_row[h] + i_row[h], -jnp.inf)
            m_t = jnp.maximum(a_col, jnp.max(dm, axis=1, keepdims=True))
            w_inter.append(jnp.exp(a_col - m_t))
            e_neg.append(jnp.exp(-m_t))
            sb.append(((qk[h] * scale) * jnp.exp(dm - m_t)).astype(BF16))

        sv = [_dot(sb[h], vb[h]) for h in heads]
        ssum = [_dot(sb[h], ones) for h in heads]

        for h in heads:
            num_scr[h] = w_inter[h] * inter[h] + sv[h]
            den = w_inter[h] * qn[h] + ssum[h]
            dn_scr[h] = jnp.maximum(jnp.abs(den), e_neg[h])
        return tuple(m_out)

    def finish(c):
        r0 = c * L if isinstance(c, int) else pl.multiple_of(c * L, L)
        for h in range(M_HEADS):
            inv = 1.0 / dn_scr[h]
            hh = num_scr[h] * jnp.concatenate([inv] * (M_V // LANES), axis=1)
            o = o_ref[0, pl.ds(r0, L), h * M_V:(h + 1) * M_V].astype(F32)
            out_ref[0, pl.ds(r0, L), h * M_V:(h + 1) * M_V] = (
                (_rms(hh) * gn_ref[h:h + 1, :]) * jax.nn.sigmoid(o)).astype(BF16)

    def pipelined(c, ms):
        finish(c - 1)
        return step(c, ms)

    ms = step(0, tuple(jnp.zeros((1, 1), F32) for _ in range(M_HEADS)))
    lax.fori_loop(1, nc, pipelined, ms)
    finish(nc - 1)


def _mlstm(z3, kt, gates_t, b_mgate, g_mnorm):
    b, s, _ = z3.shape
    nc = s // M_CHUNK
    g4 = gates_t.reshape(GATE_ROWS, b, nc, M_CHUNK)
    return pl.pallas_call(
        _mlstm_kernel,
        grid=(b,),
        in_specs=[
            pl.BlockSpec(memory_space=pltpu.SMEM),
            pl.BlockSpec((1, s, M_QK_W), lambda i: (i, 0, Z_QM // M_QK_W)),
            pl.BlockSpec((M_QK_W, s), lambda i: (0, i)),
            pl.BlockSpec((1, s, M_V_W), lambda i: (i, 0, Z_VM // M_V_W)),
            pl.BlockSpec((1, s, M_V_W), lambda i: (i, 0, Z_OM // M_V_W)),
            pl.BlockSpec((GATE_ROWS, 1, nc, M_CHUNK), lambda i: (0, i, 0, 0)),
            pl.BlockSpec((M_HEADS, M_V), lambda i: (0, 0)),
        ],
        out_specs=pl.BlockSpec((1, s, M_V_W), lambda i: (i, 0, 0)),
        out_shape=jax.ShapeDtypeStruct((b, s, M_V_W), BF16),
        scratch_shapes=[
            pltpu.VMEM((M_HEADS, nc, M_CHUNK), F32), pltpu.VMEM((M_HEADS, nc, M_CHUNK), F32),
            pltpu.VMEM((M_HEADS, M_QK, M_V), F32), pltpu.VMEM((M_HEADS, M_QK, LANES), F32),
            pltpu.VMEM((M_HEADS, M_CHUNK, M_V), F32), pltpu.VMEM((M_HEADS, M_CHUNK, LANES), F32),
        ],
        compiler_params=_params(("parallel",)),
        name="mlstm",
    )(b_mgate, z3, kt, z3, z3, g4, g_mnorm)


def _mla_proj_kernel(ql_ref, kvl_ref, kr_ref, krs_ref, gq_ref, gkv_ref, wq_ref, wqs_ref, wk_ref,
                     wv_ref, gqn_ref, gqs_ref, gkn_ref, gks_ref, cos_ref, sin_ref,
                     q_out, k_out, v_out):
    scale = QK_DIM ** -0.5 * LOG2_E
    qn = (_rms(ql_ref[...].astype(F32)) * gq_ref[...]).astype(BF16)
    kvn = (_rms(kvl_ref[...].astype(F32)) * gkv_ref[...]).astype(BF16)
    qf = _dot(qn, wq_ref[...])
    qs = _dot(qn, wqs_ref[...])
    kf = _dot(kvn, wk_ref[...])
    v_out[...] = _dot(kvn, wv_ref[...]).astype(BF16)

    cos_t = cos_ref[...]
    sin_t = sin_ref[...]
    gqn = gqn_ref[...] * scale
    gkn = gkn_ref[...]
    q_cos = gqn[:, LANES:] * cos_t
    q_sin = (gqs_ref[...] * scale) * sin_t
    kr = kr_ref[...].astype(F32)
    kr_ss = jnp.sum(kr * kr, axis=-1, keepdims=True)
    k_rot = kr * (gkn[:, LANES:] * cos_t) + krs_ref[...].astype(F32) * (gks_ref[...] * sin_t)
    for h in range(MLA_HEADS):
        qh = qf[:, h * QK_PAD:(h + 1) * QK_PAD]
        rq = lax.rsqrt(jnp.sum(qh * qh, axis=-1, keepdims=True) * (1.0 / QK_DIM) + NORM_EPS)
        q_out[:, h * QK_PAD:h * QK_PAD + LANES] = (qh[:, :LANES] * rq * gqn[:, :LANES]).astype(BF16)
        q_out[:, h * QK_PAD + LANES:(h + 1) * QK_PAD] = (
            (qh[:, LANES:] * q_cos + qs[:, h * LANES:(h + 1) * LANES] * q_sin) * rq).astype(BF16)

        kh = kf[:, h * NOPE_DIM:(h + 1) * NOPE_DIM]
        ss = jnp.sum(kh * kh, axis=-1, keepdims=True) + kr_ss
        rk = lax.rsqrt(ss * (1.0 / QK_DIM) + NORM_EPS)
        k_out[:, h * QK_PAD:h * QK_PAD + LANES] = (kh * rk * gkn[:, :LANES]).astype(BF16)
        k_out[:, h * QK_PAD + LANES:(h + 1) * QK_PAD] = (k_rot * rk).astype(BF16)


def _mla_proj(z, g_qlat, g_kvlat, w_q, w_qs, w_k, w_v, g_qn, g_qs, g_kn, g_ks, cos_t, sin_t, seq):
    t = z.shape[0]
    per_b = seq // TM_MLA
    full = lambda shape: pl.BlockSpec(shape, lambda i: (0,) * len(shape))
    return pl.pallas_call(
        _mla_proj_kernel,
        grid=(t // TM_MLA,),
        in_specs=[
            pl.BlockSpec((TM_MLA, Q_RANK), lambda i: (i, Z_QLAT // Q_RANK)),
            pl.BlockSpec((TM_MLA, KV_RANK), lambda i: (i, Z_KVLAT // KV_RANK)),
            pl.BlockSpec((TM_MLA, LANES), lambda i: (i, Z_KR // LANES)),
            pl.BlockSpec((TM_MLA, LANES), lambda i: (i, Z_KR // LANES + 1)),
            full((1, Q_RANK)), full((1, KV_RANK)),
            full(w_q.shape), full(w_qs.shape), full(w_k.shape), full(w_v.shape),
            full((1, QK_PAD)), full((1, LANES)), full((1, QK_PAD)), full((1, LANES)),
            pl.BlockSpec((TM_MLA, LANES), lambda i: (i % per_b, 0)),
            pl.BlockSpec((TM_MLA, LANES), lambda i: (i % per_b, 0)),
        ],
        out_specs=[
            pl.BlockSpec((TM_MLA, MLA_HEADS * QK_PAD), lambda i: (i, 0)),
            pl.BlockSpec((TM_MLA, MLA_HEADS * QK_PAD), lambda i: (i, 0)),
            pl.BlockSpec((TM_MLA, MLA_W), lambda i: (i, 0)),
        ],
        out_shape=[
            jax.ShapeDtypeStruct((t, MLA_HEADS * QK_PAD), BF16),
            jax.ShapeDtypeStruct((t, MLA_HEADS * QK_PAD), BF16),
            jax.ShapeDtypeStruct((t, MLA_W), BF16),
        ],
        compiler_params=_params(("parallel",)),
        name="mla_proj",
    )(z, z, z, z, g_qlat.reshape(1, Q_RANK), g_kvlat.reshape(1, KV_RANK), w_q, w_qs, w_k, w_v,
      g_qn, g_qs, g_kn, g_ks, cos_t, sin_t)


def _attn_kernel(q_ref, k_ref, v_ref, o_ref):
    s = q_ref.shape[0]
    rows = lax.broadcasted_iota(jnp.int32, (TQ, TQ), 0)
    cols = lax.broadcasted_iota(jnp.int32, (TQ, TQ), 1)
    causal = rows >= cols

    def logits(blk):
        s0, s1 = blk * TQ, (blk + 1) * TQ
        q = q_ref[s0:s1, :]
        diag = _dot_nt(q, k_ref[s0:s1, :])
        past = _dot_nt(q, k_ref[:s0, :]) if blk else None
        return diag, past

    n_blk = s // TQ
    nxt = logits(0)
    for blk in range(n_blk):
        s0, s1 = blk * TQ, (blk + 1) * TQ
        diag, past = nxt
        if blk + 1 < n_blk:
            nxt = logits(blk + 1)
        diag = jnp.where(causal, diag, -jnp.inf)
        m = jnp.max(diag, axis=-1, keepdims=True)
        if blk:
            m = jnp.maximum(m, jnp.max(past, axis=-1, keepdims=True))
        p = jnp.exp2(diag - m)
        denom = jnp.sum(p, axis=-1, keepdims=True)
        acc = _dot(p.astype(BF16), v_ref[s0:s1, :])
        if blk:
            pp = jnp.exp2(past - m)
            denom = denom + jnp.sum(pp, axis=-1, keepdims=True)
            acc = acc + _dot(pp.astype(BF16), v_ref[:s0, :])
        o_ref[s0:s1, :] = (acc * (1.0 / denom)).astype(BF16)


def _attention(q, k, v, batch, seq):
    t = q.shape[0]
    return pl.pallas_call(
        _attn_kernel,
        grid=(batch, MLA_HEADS),
        in_specs=[
            pl.BlockSpec((seq, QK_PAD), lambda b, h: (b, h)),
            pl.BlockSpec((seq, QK_PAD), lambda b, h: (b, h)),
            pl.BlockSpec((seq, MLA_V), lambda b, h: (b, h)),
        ],
        out_specs=pl.BlockSpec((seq, MLA_V), lambda b, h: (b, h)),
        out_shape=jax.ShapeDtypeStruct((t, MLA_W), BF16),
        compiler_params=_params(("parallel", "parallel")),
        name="attention",
    )(q, k, v)


def _pool_kernel(u0_ref, u1_ref, u2_ref, u3_ref, w_ref, b_ref, s_ref, o_ref):
    s = o_ref.shape[0]
    row = lax.broadcasted_iota(jnp.int32, (s, 1), 0)
    for g, (u_ref, win) in enumerate(zip((u0_ref, u1_ref, u2_ref, u3_ref), POOL_WINDOWS)):
        c0, c1 = g * POOL_GROUP_W, (g + 1) * POOL_GROUP_W
        u = u_ref[...].astype(F32)
        acc = u
        sh = 1
        while sh < win:
            acc = acc + jnp.where(row >= sh, pltpu.roll(acc, sh, axis=0), 0.0)
            sh *= 2
        inv_cnt = 1.0 / jnp.minimum(row + 1, win).astype(F32)
        pooled = (acc * inv_cnt - u).astype(BF16)
        y = _dot(pooled, w_ref[g]) + b_ref[:, c0:c1]
        o_ref[:, c0:c1] = (y * s_ref[:, c0:c1]).astype(BF16)


def _pool(z, w_pool, layer, b_pool, s_pool, batch, seq):
    t = z.shape[0]
    group = lambda g: pl.BlockSpec((seq, POOL_GROUP_W), lambda b: (b, Z_POOL // POOL_GROUP_W + g))
    return pl.pallas_call(
        _pool_kernel,
        grid=(batch,),
        in_specs=[
            group(0), group(1), group(2), group(3),
            pl.BlockSpec((None,) + w_pool.shape[1:], lambda b: (layer, 0, 0, 0)),
            pl.BlockSpec((1, POOL_W), lambda b: (0, 0)),
            pl.BlockSpec((1, POOL_W), lambda b: (0, 0)),
        ],
        out_specs=pl.BlockSpec((seq, POOL_W), lambda b: (b, 0)),
        out_shape=jax.ShapeDtypeStruct((t, POOL_W), BF16),
        compiler_params=_params(("parallel",)),
        name="pool",
    )(z, z, z, z, w_pool, b_pool.reshape(1, POOL_W), s_pool.reshape(1, POOL_W))


def _merge_kernel(a_ref, b_ref, c_ref, ga_ref, gb_ref, gc_ref, wb_ref, wo_ref, x_ref, gate_ref,
                  o_ref):
    merged = (jax.nn.sigmoid(ga_ref[...].astype(F32)) * _dot(a_ref[...], wb_ref[0])
              + jax.nn.sigmoid(gb_ref[...].astype(F32)) * _dot(b_ref[...], wb_ref[1])
              + jax.nn.sigmoid(gc_ref[...].astype(F32)) * _dot(c_ref[...], wb_ref[2]))
    o_ref[...] = x_ref[...] + gate_ref[0] * _dot(merged.astype(BF16), wo_ref[...])


def _merge(br_a, br_b, br_c, z, w_branch, w_out, layer, x2, gate1, seq):
    t, d = x2.shape
    per_b = seq // TM_MERGE
    g0 = Z_GATES // d
    br = pl.BlockSpec((TM_MERGE, BRANCH_W), lambda i: (i, 0))
    gate = lambda k: pl.BlockSpec((TM_MERGE, d), lambda i: (i, g0 + k))
    return pl.pallas_call(
        _merge_kernel,
        grid=(t // TM_MERGE,),
        in_specs=[
            br, br, br, gate(0), gate(1), gate(2),
            pl.BlockSpec((None, N_BRANCH, BRANCH_W, d), lambda i: (layer, 0, 0, 0),
                         pipeline_mode=pl.Buffered(1)),
            pl.BlockSpec((None, d, d), lambda i: (layer, 0, 0), pipeline_mode=pl.Buffered(1)),
            pl.BlockSpec((TM_MERGE, d), lambda i: (i, 0)),
            pl.BlockSpec((1, 1, d), lambda i: (i // per_b, 0, 0)),
        ],
        out_specs=pl.BlockSpec((TM_MERGE, d), lambda i: (i, 0)),
        out_shape=jax.ShapeDtypeStruct((t, d), F32),
        compiler_params=_params(("parallel",)),
        name="merge",
    )(br_a, br_b, br_c, z, z, z, w_branch, w_out, x2, gate1)


def _ffn_kernel(x_ref, g_ref, sc_ref, sh_ref, gate_ref, wg_ref, wu_ref, wo_ref, o_ref,
                h_scr, r_scr):
    j = pl.program_id(1)
    last = pl.num_programs(1) - 1

    def hidden_tile(combine):
        for r0 in range(0, TM_FFN, TM_FFN_SUB):
            rows = slice(r0, r0 + TM_FFN_SUB)
            h = h_scr[rows, :]
            gp = _dot(h, wg_ref[...])
            up = _dot(h, wu_ref[...])
            act = ((gp * jax.nn.sigmoid(gp)) * up).astype(BF16)
            combine(rows, _dot(act, wo_ref[...]))

    def write(rows, contrib):
        o_ref[rows, :] = contrib

    def add(rows, contrib):
        o_ref[rows, :] += contrib

    def add_residual(rows, contrib):
        o_ref[rows, :] = x_ref[rows, :] + gate_ref[0] * (o_ref[rows, :] + contrib)

    @pl.when(j == 0)
    def _():
        _norm_modulate_into(x_ref, g_ref, sc_ref, sh_ref, h_scr, r_scr)
        hidden_tile(write)

    @pl.when(jnp.logical_and(j > 0, j < last))
    def _():
        hidden_tile(add)

    @pl.when(j == last)
    def _():
        hidden_tile(add_residual)


def _ffn(x2, g, scale, shift, gate, w_in, w_out, layer, seq):
    t, d = x2.shape
    per_b = seq // TM_FFN
    nj = FFN_DIM // TF_FFN
    assert nj >= 2, "the first and the last hidden tile must be different grid steps"
    vec = pl.BlockSpec((1, 1, d), lambda i, j: (i // per_b, 0, 0))
    return pl.pallas_call(
        _ffn_kernel,
        grid=(t // TM_FFN, nj),
        in_specs=[
            pl.BlockSpec((TM_FFN, d), lambda i, j: (i, 0)),
            pl.BlockSpec((1, d), lambda i, j: (0, 0)),
            vec, vec, vec,
            pl.BlockSpec((None, d, TF_FFN), lambda i, j: (layer, 0, j)),
            pl.BlockSpec((None, d, TF_FFN), lambda i, j: (layer, 0, nj + j)),
            pl.BlockSpec((None, TF_FFN, d), lambda i, j: (layer, j, 0)),
        ],
        out_specs=pl.BlockSpec((TM_FFN, d), lambda i, j: (i, 0)),
        out_shape=jax.ShapeDtypeStruct((t, d), F32),
        scratch_shapes=[pltpu.VMEM((TM_FFN, d), BF16), pltpu.VMEM((TM_FFN, LANES), F32)],
        compiler_params=_params(("parallel", "arbitrary")),
        name="ffn",
    )(x2, g.reshape(1, d), scale, shift, gate, w_in, w_in, w_out)


def _rope_lanes(a, axis):
    x1, x2 = jnp.split(a, 2, axis=axis)
    zero = jnp.zeros_like(x1)
    return jnp.concatenate([x1, zero, x2, zero], axis=axis)


def _in_offsets():
    offs, off = [], 0
    for size in IN_SIZES:
        offs.append(off)
        off += size
    return offs


(IN_QM, IN_KM, IN_VM, IN_OM, IN_IM, IN_FM, IN_QLAT, IN_KVLAT, IN_KR, IN_POOL, IN_GATES) = _in_offsets()
IN_W = sum(IN_SIZES)

_Z_COPIES = ((Z_GATES, IN_GATES, N_BRANCH * D_MODEL), (Z_QM, IN_QM, M_QK_W), (Z_QLAT, IN_QLAT, Q_RANK),
             (Z_VM, IN_VM, M_V_W), (Z_OM, IN_OM, M_V_W), (Z_KVLAT, IN_KVLAT, KV_RANK),
             (Z_POOL, IN_POOL, POOL_W))
PREP_COLS = 256
PREP_ROWS = 512


def _w_prep_kernel(w_ref, o_ref, kt_ref, if_ref):
    cols = w_ref.shape[1]
    kt_ref[...] = w_ref[IN_KM:IN_KM + M_QK_W, :].astype(BF16)
    if_ref[...] = jnp.concatenate(
        [w_ref[IN_IM:IN_IM + 2 * M_HEADS, :], jnp.zeros((GATE_ROWS - 2 * M_HEADS, cols), F32)],
        axis=0).astype(BF16)
    for dst, src, width in _Z_COPIES:
        for r in range(0, width, PREP_ROWS):
            n = min(PREP_ROWS, width - r)
            o_ref[dst + r:dst + r + n, :] = w_ref[src + r:src + r + n, :].astype(BF16)
    zero = jnp.zeros((ROPE_HALF, cols), BF16)
    x1 = w_ref[IN_KR:IN_KR + ROPE_HALF, :].astype(BF16)
    x2 = w_ref[IN_KR + ROPE_HALF:IN_KR + ROPE_DIM, :].astype(BF16)
    for k, part in enumerate((x1, zero, x2, zero, x2, zero, x1, zero)):
        o_ref[Z_KR + k * ROPE_HALF:Z_KR + (k + 1) * ROPE_HALF, :] = part


def _prep_w_in(w_in):
    depth, d, _ = w_in.shape
    return pl.pallas_call(
        _w_prep_kernel,
        grid=(depth, d // PREP_COLS),
        in_specs=[pl.BlockSpec((None, IN_W, PREP_COLS), lambda l, c: (l, 0, c))],
        out_specs=[
            pl.BlockSpec((None, Z_W, PREP_COLS), lambda l, c: (l, 0, c)),
            pl.BlockSpec((None, M_QK_W, PREP_COLS), lambda l, c: (l, 0, c)),
            pl.BlockSpec((None, GATE_ROWS, PREP_COLS), lambda l, c: (l, 0, c)),
        ],
        out_shape=[
            jax.ShapeDtypeStruct((depth, Z_W, d), BF16),
            jax.ShapeDtypeStruct((depth, M_QK_W, d), BF16),
            jax.ShapeDtypeStruct((depth, GATE_ROWS, d), BF16),
        ],
        compiler_params=_params(("parallel", "parallel")),
        name="w_prep",
    )(jnp.swapaxes(w_in, 1, 2))


def _prep_head_cols(a):
    return jnp.concatenate([a[..., :NOPE_DIM], _rope_lanes(a[..., NOPE_DIM:], -1)], axis=-1)


def _prep_rope_swapped(a):
    return _rope_lanes(
        jnp.concatenate([a[..., NOPE_DIM + ROPE_HALF:], a[..., NOPE_DIM:NOPE_DIM + ROPE_HALF]],
                        axis=-1), -1)


def _rope_tables(seq):
    pos = jnp.arange(seq, dtype=F32)
    freqs = ROPE_THETA ** (-jnp.arange(0, ROPE_DIM, 2, dtype=F32) / ROPE_DIM)
    ang = pos[:, None] * freqs[None, :]
    cos, sin = jnp.cos(ang), jnp.sin(ang)
    zero = jnp.zeros_like(cos)
    cos_t = jnp.concatenate([cos, zero, cos, zero], axis=1)
    sin_t = jnp.concatenate([-sin, zero, sin, zero], axis=1)
    return cos_t, sin_t


def kernel(x, c, w_ada, b_ada, g_norm1, w_in, b_mgate, g_mnorm, g_qlat, w_uq, g_kvlat, w_ukv,
           g_qn, g_kn, w_pool, b_pool, s_pool, w_branch, w_out, g_norm2, w_ffn_in, w_ffn_out):
    batch, seq, d = x.shape
    depth = w_in.shape[0]
    t = batch * seq

    mod = _ada(c, w_ada, b_ada)
    cos_t, sin_t = _rope_tables(seq)
    x2 = x.reshape(t, d)
    w_pool_b, w_branch_b, w_out_b = w_pool.astype(BF16), w_branch.astype(BF16), w_out.astype(BF16)
    w_ffn_in_b, w_ffn_out_b = w_ffn_in.astype(BF16), w_ffn_out.astype(BF16)
    w_z, w_kt, w_if = _prep_w_in(w_in)

    for l in range(depth):
        shift1, scale1, gate1, shift2, scale2, gate2 = [
            mod[l, :, k * d:(k + 1) * d].reshape(batch, 1, d) for k in range(6)]

        z, kt, gates_t = _inproj(x2, g_norm1[l], scale1, shift1, w_z, l, w_kt, w_if, seq)

        br_a = _mlstm(z.reshape(batch, seq, Z_W), kt, gates_t, b_mgate[l],
                      g_mnorm[l]).reshape(t, M_V_W)

        w_uq_h = w_uq[l].reshape(Q_RANK, MLA_HEADS, QK_DIM)
        w_q = _prep_head_cols(w_uq_h).reshape(Q_RANK, MLA_HEADS * QK_PAD).astype(BF16)
        w_qs = _prep_rope_swapped(w_uq_h).reshape(Q_RANK, MLA_HEADS * LANES).astype(BF16)
        w_kv = w_ukv[l].reshape(KV_RANK, MLA_HEADS, NOPE_DIM + MLA_V)
        w_k = w_kv[..., :NOPE_DIM].reshape(KV_RANK, MLA_HEADS * NOPE_DIM).astype(BF16)
        w_v = w_kv[..., NOPE_DIM:].reshape(KV_RANK, MLA_W).astype(BF16)
        q, k, v = _mla_proj(z, g_qlat[l], g_kvlat[l], w_q, w_qs, w_k, w_v,
                            _prep_head_cols(g_qn[l]).reshape(1, QK_PAD),
                            _prep_rope_swapped(g_qn[l]).reshape(1, LANES),
                            _prep_head_cols(g_kn[l]).reshape(1, QK_PAD),
                            _prep_rope_swapped(g_kn[l]).reshape(1, LANES), cos_t, sin_t, seq)
        br_b = _attention(q, k, v, batch, seq)

        br_c = _pool(z, w_pool_b, l, b_pool[l], s_pool[l], batch, seq)

        x2 = _merge(br_a, br_b, br_c, z, w_branch_b, w_out_b, l, x2, gate1, seq)
        x2 = _ffn(x2, g_norm2[l], scale2, shift2, gate2, w_ffn_in_b, w_ffn_out_b, l, seq)

    return x2.reshape(batch, seq, d)
```

```python
import functools

import jax
import jax.numpy as jnp
from jax import lax
from jax.experimental import pallas as pl
from jax.experimental.pallas import tpu as pltpu

F32 = jnp.float32
BF16 = jnp.bfloat16

D_MODEL = 2048
M_HEADS = 4
M_QK = 128
M_V = 256
M_QK_W = M_HEADS * M_QK
M_V_W = M_HEADS * M_V
M_CHUNK = 128
MLA_HEADS = 8
NOPE_DIM = 128
ROPE_DIM = 64
ROPE_HALF = ROPE_DIM // 2
QK_DIM = NOPE_DIM + ROPE_DIM
MLA_V = 128
MLA_W = MLA_HEADS * MLA_V
Q_RANK = 512
KV_RANK = 256
ROPE_THETA = 10000.0
POOL_WINDOWS = (2, 4, 8, 16)
POOL_GROUPS = 4
POOL_GROUP_W = 256
POOL_W = POOL_GROUPS * POOL_GROUP_W
N_BRANCH = 3
BRANCH_W = 1024
FFN_DIM = ((8 * D_MODEL // 3 + 255) // 256) * 256
NORM_EPS = 1e-6
LOG2_E = 1.4426950408889634
IN_SIZES = (M_QK_W, M_QK_W, M_V_W, M_V_W, M_HEADS, M_HEADS, Q_RANK, KV_RANK, ROPE_DIM, POOL_W,
            N_BRANCH * D_MODEL)

LANES = 128
QK_PAD = 2 * LANES
GATE_ROWS = 16

Z_GATES = 0
Z_QM = Z_GATES + N_BRANCH * D_MODEL
Z_QLAT = Z_QM + M_QK_W
Z_VM = Z_QLAT + Q_RANK
Z_OM = Z_VM + M_V_W
Z_KVLAT = Z_OM + M_V_W
Z_KR = Z_KVLAT + KV_RANK
Z_POOL = Z_KR + 2 * LANES
Z_W = Z_POOL + POOL_W

V7X_VMEM_LIMIT = 52 * 1024 * 1024

TM_IN = 1024
TN_IN = 1536
NORM_ROWS = 64
NORM_UNROLL = 4
TM_MLA = 1024
TQ = 256
TM_MERGE = 256
TM_FFN = 1024
TM_FFN_SUB = 512
TF_FFN = 512
TN_ADA = 1024


def _params(semantics):
    return pltpu.CompilerParams(dimension_semantics=semantics, vmem_limit_bytes=V7X_VMEM_LIMIT)


def _rms(x, width=None):
    ss = jnp.sum(x * x, axis=-1, keepdims=True)
    n = x.shape[-1] if width is None else width
    return x * lax.rsqrt(ss * (1.0 / n) + NORM_EPS)


def _dot(a, b):
    return jnp.dot(a, b, preferred_element_type=F32)


def _dot_nt(a, b):
    return lax.dot_general(a, b, (((1,), (1,)), ((), ())), preferred_element_type=F32)


def _ada_kernel(c_ref, w_ref, b_ref, o_ref):
    c = c_ref[...]
    ca = (c * jax.nn.sigmoid(c)).astype(BF16)
    o_ref[0] = _dot(ca, w_ref[0].astype(BF16)) + b_ref[0]


def _ada(c, w_ada, b_ada):
    depth, d, n = w_ada.shape
    b = c.shape[0]
    return pl.pallas_call(
        _ada_kernel,
        grid=(depth, n // TN_ADA),
        in_specs=[
            pl.BlockSpec((b, d), lambda l, j: (0, 0)),
            pl.BlockSpec((1, d, TN_ADA), lambda l, j: (l, 0, j)),
            pl.BlockSpec((1, 1, TN_ADA), lambda l, j: (l, 0, j)),
        ],
        out_specs=pl.BlockSpec((1, b, TN_ADA), lambda l, j: (l, 0, j)),
        out_shape=jax.ShapeDtypeStruct((depth, b, n), F32),
        compiler_params=_params(("parallel", "parallel")),
        name="adaln",
    )(c, w_ada, b_ada.reshape(depth, 1, n))


def _norm_modulate_into(x_ref, g_ref, sc_ref, sh_ref, h_scr, r_scr, row0, rows, straight_line):
    n_slabs = rows // NORM_ROWS
    d = x_ref.shape[1]

    def slab(r):
        if isinstance(r, int):
            return slice(row0 + r * NORM_ROWS, row0 + (r + 1) * NORM_ROWS)
        return pl.ds(pl.multiple_of(row0 + r * NORM_ROWS, NORM_ROWS), NORM_ROWS)

    def stats(r, carry):
        x = x_ref[slab(r), :]
        ms = jnp.sum(x * x, axis=-1, keepdims=True) * (1.0 / d)
        r_scr[slab(r), :] = jnp.broadcast_to(lax.rsqrt(ms + NORM_EPS), (NORM_ROWS, LANES))
        return carry

    gain = g_ref[...] * (1.0 + sc_ref[0])
    shift = sh_ref[0]

    def apply(r, carry):
        rstd = r_scr[slab(r), :]
        for t in range(d // LANES):
            c0, c1 = t * LANES, (t + 1) * LANES
            x = x_ref[slab(r), c0:c1]
            h_scr[slab(r), c0:c1] = (x * rstd * gain[:, c0:c1] + shift[:, c0:c1]).astype(BF16)
        return carry

    if straight_line:
        for r in range(n_slabs):
            stats(r, 0)
        for r in range(n_slabs):
            apply(r, 0)
    else:
        lax.fori_loop(0, n_slabs, stats, 0, unroll=NORM_UNROLL)
        lax.fori_loop(0, n_slabs, apply, 0)


def _inproj_kernel(x_ref, g_ref, sc_ref, sh_ref, w_ref, wkt_ref, wif_ref, z_ref, kt_ref, gt_ref,
                   h_scr, r_scr):
    j = pl.program_id(1)
    half = x_ref.shape[0] // 2

    def project(rows):
        h = h_scr[rows, :]
        kt_ref[:, rows] = _dot_nt(wkt_ref[...], h).astype(BF16)
        gt_ref[:, rows] = _dot_nt(wif_ref[...], h)
        z_ref[rows, :] = _dot_nt(h, w_ref[...]).astype(BF16)

    @pl.when(j == 0)
    def _():
        _norm_modulate_into(x_ref, g_ref, sc_ref, sh_ref, h_scr, r_scr, 0, half, False)
        _norm_modulate_into(x_ref, g_ref, sc_ref, sh_ref, h_scr, r_scr, half, half, True)
        project(slice(0, half))
        project(slice(half, 2 * half))

    @pl.when(j > 0)
    def _():
        z_ref[...] = _dot_nt(h_scr[...], w_ref[...]).astype(BF16)


def _inproj(x2, g, scale, shift, w_z, layer, w_kt, w_if, seq):
    t, d = x2.shape
    per_b = seq // TM_IN
    vec = pl.BlockSpec((1, 1, d), lambda i, j: (i // per_b, 0, 0))
    return pl.pallas_call(
        _inproj_kernel,
        grid=(t // TM_IN, Z_W // TN_IN),
        in_specs=[
            pl.BlockSpec((TM_IN, d), lambda i, j: (i, 0)),
            pl.BlockSpec((1, d), lambda i, j: (0, 0)),
            vec, vec,
            pl.BlockSpec((None, TN_IN, d), lambda i, j: (layer, j, 0)),
            pl.BlockSpec((None, M_QK_W, d), lambda i, j: (layer, 0, 0)),
            pl.BlockSpec((None, GATE_ROWS, d), lambda i, j: (layer, 0, 0)),
        ],
        out_specs=[
            pl.BlockSpec((TM_IN, TN_IN), lambda i, j: (i, j)),
            pl.BlockSpec((M_QK_W, TM_IN), lambda i, j: (0, i)),
            pl.BlockSpec((GATE_ROWS, TM_IN), lambda i, j: (0, i)),
        ],
        out_shape=[
            jax.ShapeDtypeStruct((t, Z_W), BF16),
            jax.ShapeDtypeStruct((M_QK_W, t), BF16),
            jax.ShapeDtypeStruct((GATE_ROWS, t), F32),
        ],
        scratch_shapes=[pltpu.VMEM((TM_IN, d), BF16), pltpu.VMEM((TM_IN, LANES), F32)],
        compiler_params=_params(("parallel", "arbitrary")),
        name="inproj",
    )(x2, g.reshape(1, d), scale, shift, w_z, w_kt, w_if)


def _mlstm_kernel(bias_ref, q_ref, kt_ref, v_ref, o_ref, g_ref, gn_ref, out_ref,
                  b_scr, i_scr, ct_scr, nr_scr, num_scr, dn_scr):
    L = M_CHUNK
    nc = g_ref.shape[2]
    scale = M_QK ** -0.5

    lane = lax.broadcasted_iota(jnp.int32, (nc, L), 1)
    for h in range(M_HEADS):
        fg = g_ref[M_HEADS + h, 0] + bias_ref[1, h]
        bc = jnp.minimum(fg, 0.0) - jnp.log(1.0 + jnp.exp(-jnp.abs(fg)))
        sh = 1
        while sh < L:
            bc = bc + jnp.where(lane >= sh, pltpu.roll(bc, sh, axis=1), 0.0)
            sh *= 2
        b_scr[h] = bc
        i_scr[h] = g_ref[h, 0] + bias_ref[0, h]
    ct_scr[...] = jnp.zeros_like(ct_scr)
    nr_scr[...] = jnp.zeros_like(nr_scr)

    rows = lax.broadcasted_iota(jnp.int32, (L, L), 0)
    cols = lax.broadcasted_iota(jnp.int32, (L, L), 1)
    eye = rows == cols
    tril = rows >= cols
    ones = jnp.ones((L, LANES), BF16)

    def to_col(row):
        return jnp.sum(jnp.where(eye, jnp.broadcast_to(row, (L, L)), 0.0), axis=1, keepdims=True)

    def step(c, ms):
        r0 = c * L if isinstance(c, int) else pl.multiple_of(c * L, L)
        heads = range(M_HEADS)
        qb = [q_ref[0, pl.ds(r0, L), h * M_QK:(h + 1) * M_QK] for h in heads]
        kt = [kt_ref[h * M_QK:(h + 1) * M_QK, pl.ds(r0, L)] for h in heads]
        vb = [v_ref[0, pl.ds(r0, L), h * M_V:(h + 1) * M_V] for h in heads]
        b_row = [b_scr[h, pl.ds(c, 1), :] for h in heads]
        i_row = [i_scr[h, pl.ds(c, 1), :] for h in heads]
        ct = [ct_scr[h] for h in heads]
        nr = [nr_scr[h] for h in heads]

        qk = [_dot(qb[h], kt[h]) for h in heads]
        inter = [_dot(qb[h], ct[h].astype(BF16)) for h in heads]
        qn = [_dot(qb[h], nr[h].astype(BF16)) for h in heads]

        m_out = []
        for h in heads:
            b_last = b_row[h][:, L - 1:L]
            g_row = b_last - b_row[h] + i_row[h]
            m_new = jnp.maximum(b_last + ms[h], jnp.max(g_row, axis=1, keepdims=True))
            decay = jnp.exp(b_last + ms[h] - m_new)
            ktw = (kt[h].astype(F32) * (jnp.exp(g_row - m_new) * scale)).astype(BF16)
            ct_scr[h] = decay * ct[h] + _dot(ktw, vb[h])
            nr_scr[h] = decay * nr[h] + _dot(ktw, ones)
            m_out.append(m_new)

        w_inter, e_neg, sb = [], [], []
        for h in heads:
            b_col = to_col(b_row[h])
            a_col = b_col + ms[h]
            dm = jnp.where(tril, b_col - b_row[h] + i_row[h], -jnp.inf)
            m_t = jnp.maximum(a_col, jnp.max(dm, axis=1, keepdims=True))
            w_inter.append(jnp.exp(a_col - m_t))
            e_neg.append(jnp.exp(-m_t))
            sb.append(((qk[h] * scale) * jnp.exp(dm - m_t)).astype(BF16))

        sv = [_dot(sb[h], vb[h]) for h in heads]
        ssum = [_dot(sb[h], ones) for h in heads]

        for h in heads:
            num_scr[h] = w_inter[h] * inter[h] + sv[h]
            den = w_inter[h] * qn[h] + ssum[h]
            dn_scr[h] = jnp.maximum(jnp.abs(den), e_neg[h])
        return tuple(m_out)

    def finish(c):
        r0 = c * L if isinstance(c, int) else pl.multiple_of(c * L, L)
        for h in range(M_HEADS):
            inv = 1.0 / dn_scr[h]
            hh = num_scr[h] * jnp.concatenate([inv] * (M_V // LANES), axis=1)
            o = o_ref[0, pl.ds(r0, L), h * M_V:(h + 1) * M_V].astype(F32)
            out_ref[0, pl.ds(r0, L), h * M_V:(h + 1) * M_V] = (
                (_rms(hh) * gn_ref[h:h + 1, :]) * jax.nn.sigmoid(o)).astype(BF16)

    def pipelined(c, ms):
        finish(c - 1)
        return step(c, ms)

    ms = step(0, tuple(jnp.zeros((1, 1), F32) for _ in range(M_HEADS)))
    lax.fori_loop(1, nc, pipelined, ms)
    finish(nc - 1)


def _mlstm(z3, kt, gates_t, b_mgate, g_mnorm):
    b, s, _ = z3.shape
    nc = s // M_CHUNK
    g4 = gates_t.reshape(GATE_ROWS, b, nc, M_CHUNK)
    return pl.pallas_call(
        _mlstm_kernel,
        grid=(b,),
        in_specs=[
            pl.BlockSpec(memory_space=pltpu.SMEM),
            pl.BlockSpec((1, s, M_QK_W), lambda i: (i, 0, Z_QM // M_QK_W)),
            pl.BlockSpec((M_QK_W, s), lambda i: (0, i)),
            pl.BlockSpec((1, s, M_V_W), lambda i: (i, 0, Z_VM // M_V_W)),
            pl.BlockSpec((1, s, M_V_W), lambda i: (i, 0, Z_OM // M_V_W)),
            pl.BlockSpec((GATE_ROWS, 1, nc, M_CHUNK), lambda i: (0, i, 0, 0)),
            pl.BlockSpec((M_HEADS, M_V), lambda i: (0, 0)),
        ],
        out_specs=pl.BlockSpec((1, s, M_V_W), lambda i: (i, 0, 0)),
        out_shape=jax.ShapeDtypeStruct((b, s, M_V_W), BF16),
        scratch_shapes=[
            pltpu.VMEM((M_HEADS, nc, M_CHUNK), F32), pltpu.VMEM((M_HEADS, nc, M_CHUNK), F32),
            pltpu.VMEM((M_HEADS, M_QK, M_V), F32), pltpu.VMEM((M_HEADS, M_QK, LANES), F32),
            pltpu.VMEM((M_HEADS, M_CHUNK, M_V), F32), pltpu.VMEM((M_HEADS, M_CHUNK, LANES), F32),
        ],
        compiler_params=_params(("parallel",)),
        name="mlstm",
    )(b_mgate, z3, kt, z3, z3, g4, g_mnorm)


def _mla_proj_kernel(ql_ref, kvl_ref, kr_ref, krs_ref, gq_ref, gkv_ref, wq_ref, wqs_ref, wk_ref,
                     wv_ref, gqn_ref, gqs_ref, gkn_ref, gks_ref, cos_ref, sin_ref,
                     q_out, k_out, v_out):
    scale = QK_DIM ** -0.5 * LOG2_E
    qn = (_rms(ql_ref[...].astype(F32)) * gq_ref[...]).astype(BF16)
    kvn = (_rms(kvl_ref[...].astype(F32)) * gkv_ref[...]).astype(BF16)
    qf = _dot(qn, wq_ref[...])
    qs = _dot(qn, wqs_ref[...])
    kf = _dot(kvn, wk_ref[...])
    v_out[...] = _dot(kvn, wv_ref[...]).astype(BF16)

    cos_t = cos_ref[...]
    sin_t = sin_ref[...]
    gqn = gqn_ref[...] * scale
    gkn = gkn_ref[...]
    q_cos = gqn[:, LANES:] * cos_t
    q_sin = (gqs_ref[...] * scale) * sin_t
    kr = kr_ref[...].astype(F32)
    kr_ss = jnp.sum(kr * kr, axis=-1, keepdims=True)
    k_rot = kr * (gkn[:, LANES:] * cos_t) + krs_ref[...].astype(F32) * (gks_ref[...] * sin_t)
    for h in range(MLA_HEADS):
        qh = qf[:, h * QK_PAD:(h + 1) * QK_PAD]
        rq = lax.rsqrt(jnp.sum(qh * qh, axis=-1, keepdims=True) * (1.0 / QK_DIM) + NORM_EPS)
        q_out[:, h * QK_PAD:h * QK_PAD + LANES] = (qh[:, :LANES] * rq * gqn[:, :LANES]).astype(BF16)
        q_out[:, h * QK_PAD + LANES:(h + 1) * QK_PAD] = (
            (qh[:, LANES:] * q_cos + qs[:, h * LANES:(h + 1) * LANES] * q_sin) * rq).astype(BF16)

        kh = kf[:, h * NOPE_DIM:(h + 1) * NOPE_DIM]
        ss = jnp.sum(kh * kh, axis=-1, keepdims=True) + kr_ss
        rk = lax.rsqrt(ss * (1.0 / QK_DIM) + NORM_EPS)
        k_out[:, h * QK_PAD:h * QK_PAD + LANES] = (kh * rk * gkn[:, :LANES]).astype(BF16)
        k_out[:, h * QK_PAD + LANES:(h + 1) * QK_PAD] = (k_rot * rk).astype(BF16)


def _mla_proj(z, g_qlat, g_kvlat, w_q, w_qs, w_k, w_v, g_qn, g_qs, g_kn, g_ks, cos_t, sin_t, seq):
    t = z.shape[0]
    per_b = seq // TM_MLA
    full = lambda shape: pl.BlockSpec(shape, lambda i: (0,) * len(shape))
    return pl.pallas_call(
        _mla_proj_kernel,
        grid=(t // TM_MLA,),
        in_specs=[
            pl.BlockSpec((TM_MLA, Q_RANK), lambda i: (i, Z_QLAT // Q_RANK)),
            pl.BlockSpec((TM_MLA, KV_RANK), lambda i: (i, Z_KVLAT // KV_RANK)),
            pl.BlockSpec((TM_MLA, LANES), lambda i: (i, Z_KR // LANES)),
            pl.BlockSpec((TM_MLA, LANES), lambda i: (i, Z_KR // LANES + 1)),
            full((1, Q_RANK)), full((1, KV_RANK)),
            full(w_q.shape), full(w_qs.shape), full(w_k.shape), full(w_v.shape),
            full((1, QK_PAD)), full((1, LANES)), full((1, QK_PAD)), full((1, LANES)),
            pl.BlockSpec((TM_MLA, LANES), lambda i: (i % per_b, 0)),
            pl.BlockSpec((TM_MLA, LANES), lambda i: (i % per_b, 0)),
        ],
        out_specs=[
            pl.BlockSpec((TM_MLA, MLA_HEADS * QK_PAD), lambda i: (i, 0)),
            pl.BlockSpec((TM_MLA, MLA_HEADS * QK_PAD), lambda i: (i, 0)),
            pl.BlockSpec((TM_MLA, MLA_W), lambda i: (i, 0)),
        ],
        out_shape=[
            jax.ShapeDtypeStruct((t, MLA_HEADS * QK_PAD), BF16),
            jax.ShapeDtypeStruct((t, MLA_HEADS * QK_PAD), BF16),
            jax.ShapeDtypeStruct((t, MLA_W), BF16),
        ],
        compiler_params=_params(("parallel",)),
        name="mla_proj",
    )(z, z, z, z, g_qlat.reshape(1, Q_RANK), g_kvlat.reshape(1, KV_RANK), w_q, w_qs, w_k, w_v,
      g_qn, g_qs, g_kn, g_ks, cos_t, sin_t)


def _attn_kernel(q_ref, k_ref, v_ref, o_ref):
    s = q_ref.shape[0]
    rows = lax.broadcasted_iota(jnp.int32, (TQ, TQ), 0)
    cols = lax.broadcasted_iota(jnp.int32, (TQ, TQ), 1)
    causal = rows >= cols

    def logits(blk):
        s0, s1 = blk * TQ, (blk + 1) * TQ
        q = q_ref[s0:s1, :]
        diag = _dot_nt(q, k_ref[s0:s1, :])
        past = _dot_nt(q, k_ref[:s0, :]) if blk else None
        return diag, past

    n_blk = s // TQ
    nxt = logits(0)
    for blk in range(n_blk):
        s0, s1 = blk * TQ, (blk + 1) * TQ
        diag, past = nxt
        if blk + 1 < n_blk:
            nxt = logits(blk + 1)
        diag = jnp.where(causal, diag, -jnp.inf)
        m = jnp.max(diag, axis=-1, keepdims=True)
        if blk:
            m = jnp.maximum(m, jnp.max(past, axis=-1, keepdims=True))
        p = jnp.exp2(diag - m)
        denom = jnp.sum(p, axis=-1, keepdims=True)
        acc = _dot(p.astype(BF16), v_ref[s0:s1, :])
        if blk:
            pp = jnp.exp2(past - m)
            denom = denom + jnp.sum(pp, axis=-1, keepdims=True)
            acc = acc + _dot(pp.astype(BF16), v_ref[:s0, :])
        o_ref[s0:s1, :] = (acc * (1.0 / denom)).astype(BF16)


def _attention(q, k, v, batch, seq):
    t = q.shape[0]
    return pl.pallas_call(
        _attn_kernel,
        grid=(batch, MLA_HEADS),
        in_specs=[
            pl.BlockSpec((seq, QK_PAD), lambda b, h: (b, h)),
            pl.BlockSpec((seq, QK_PAD), lambda b, h: (b, h)),
            pl.BlockSpec((seq, MLA_V), lambda b, h: (b, h)),
        ],
        out_specs=pl.BlockSpec((seq, MLA_V), lambda b, h: (b, h)),
        out_shape=jax.ShapeDtypeStruct((t, MLA_W), BF16),
        compiler_params=_params(("parallel", "parallel")),
        name="attention",
    )(q, k, v)


def _pool_kernel(u0_ref, u1_ref, u2_ref, u3_ref, w_ref, b_ref, s_ref, o_ref):
    s = o_ref.shape[0]
    row = lax.broadcasted_iota(jnp.int32, (s, 1), 0)
    for g, (u_ref, win) in enumerate(zip((u0_ref, u1_ref, u2_ref, u3_ref), POOL_WINDOWS)):
        c0, c1 = g * POOL_GROUP_W, (g + 1) * POOL_GROUP_W
        u = u_ref[...].astype(F32)
        acc = u
        sh = 1
        while sh < win:
            acc = acc + jnp.where(row >= sh, pltpu.roll(acc, sh, axis=0), 0.0)
            sh *= 2
        inv_cnt = 1.0 / jnp.minimum(row + 1, win).astype(F32)
        pooled = (acc * inv_cnt - u).astype(BF16)
        y = _dot(pooled, w_ref[g]) + b_ref[:, c0:c1]
        o_ref[:, c0:c1] = (y * s_ref[:, c0:c1]).astype(BF16)


def _pool(z, w_pool, layer, b_pool, s_pool, batch, seq):
    t = z.shape[0]
    group = lambda g: pl.BlockSpec((seq, POOL_GROUP_W), lambda b: (b, Z_POOL // POOL_GROUP_W + g))
    return pl.pallas_call(
        _pool_kernel,
        grid=(batch,),
        in_specs=[
            group(0), group(1), group(2), group(3),
            pl.BlockSpec((None,) + w_pool.shape[1:], lambda b: (layer, 0, 0, 0)),
            pl.BlockSpec((1, POOL_W), lambda b: (0, 0)),
            pl.BlockSpec((1, POOL_W), lambda b: (0, 0)),
        ],
        out_specs=pl.BlockSpec((seq, POOL_W), lambda b: (b, 0)),
        out_shape=jax.ShapeDtypeStruct((t, POOL_W), BF16),
        compiler_params=_params(("parallel",)),
        name="pool",
    )(z, z, z, z, w_pool, b_pool.reshape(1, POOL_W), s_pool.reshape(1, POOL_W))


def _merge_kernel(a_ref, b_ref, c_ref, ga_ref, gb_ref, gc_ref, wb_ref, wo_ref, x_ref, gate_ref,
                  o_ref):
    merged = (jax.nn.sigmoid(ga_ref[...].astype(F32)) * _dot(a_ref[...], wb_ref[0])
              + jax.nn.sigmoid(gb_ref[...].astype(F32)) * _dot(b_ref[...], wb_ref[1])
              + jax.nn.sigmoid(gc_ref[...].astype(F32)) * _dot(c_ref[...], wb_ref[2]))
    o_ref[...] = x_ref[...] + gate_ref[0] * _dot(merged.astype(BF16), wo_ref[...])


def _merge(br_a, br_b, br_c, z, w_branch, w_out, layer, x2, gate1, seq):
    t, d = x2.shape
    per_b = seq // TM_MERGE
    g0 = Z_GATES // d
    br = pl.BlockSpec((TM_MERGE, BRANCH_W), lambda i: (i, 0))
    gate = lambda k: pl.BlockSpec((TM_MERGE, d), lambda i: (i, g0 + k))
    return pl.pallas_call(
        _merge_kernel,
        grid=(t // TM_MERGE,),
        in_specs=[
            br, br, br, gate(0), gate(1), gate(2),
            pl.BlockSpec((None, N_BRANCH, BRANCH_W, d), lambda i: (layer, 0, 0, 0),
                         pipeline_mode=pl.Buffered(1)),
            pl.BlockSpec((None, d, d), lambda i: (layer, 0, 0), pipeline_mode=pl.Buffered(1)),
            pl.BlockSpec((TM_MERGE, d), lambda i: (i, 0)),
            pl.BlockSpec((1, 1, d), lambda i: (i // per_b, 0, 0)),
        ],
        out_specs=pl.BlockSpec((TM_MERGE, d), lambda i: (i, 0)),
        out_shape=jax.ShapeDtypeStruct((t, d), F32),
        compiler_params=_params(("parallel",)),
        name="merge",
    )(br_a, br_b, br_c, z, z, z, w_branch, w_out, x2, gate1)


def _ffn_kernel(x_ref, g_ref, sc_ref, sh_ref, gate_ref, wg_ref, wu_ref, wo_ref, o_ref,
                h_scr, r_scr):
    j = pl.program_id(1)
    last = pl.num_programs(1) - 1

    def hidden_tile(combine):
        for r0 in range(0, TM_FFN, TM_FFN_SUB):
            rows = slice(r0, r0 + TM_FFN_SUB)
            h = h_scr[rows, :]
            gp = _dot(h, wg_ref[...])
            up = _dot(h, wu_ref[...])
            act = ((gp * jax.nn.sigmoid(gp)) * up).astype(BF16)
            combine(rows, _dot(act, wo_ref[...]))

    def write(rows, contrib):
        o_ref[rows, :] = contrib

    def add(rows, contrib):
        o_ref[rows, :] += contrib

    def add_residual(rows, contrib):
        o_ref[rows, :] = x_ref[rows, :] + gate_ref[0] * (o_ref[rows, :] + contrib)

    @pl.when(j == 0)
    def _():
        _norm_modulate_into(x_ref, g_ref, sc_ref, sh_ref, h_scr, r_scr, 0, TM_FFN_SUB, False)
        _norm_modulate_into(x_ref, g_ref, sc_ref, sh_ref, h_scr, r_scr, TM_FFN_SUB,
                            TM_FFN - TM_FFN_SUB, True)
        hidden_tile(write)

    @pl.when(jnp.logical_and(j > 0, j < last))
    def _():
        hidden_tile(add)

    @pl.when(j == last)
    def _():
        hidden_tile(add_residual)


def _ffn(x2, g, scale, shift, gate, w_in, w_out, layer, seq):
    t, d = x2.shape
    per_b = seq // TM_FFN
    nj = FFN_DIM // TF_FFN
    assert nj >= 2, "the first and the last hidden tile must be different grid steps"
    vec = pl.BlockSpec((1, 1, d), lambda i, j: (i // per_b, 0, 0))
    return pl.pallas_call(
        _ffn_kernel,
        grid=(t // TM_FFN, nj),
        in_specs=[
            pl.BlockSpec((TM_FFN, d), lambda i, j: (i, 0)),
            pl.BlockSpec((1, d), lambda i, j: (0, 0)),
            vec, vec, vec,
            pl.BlockSpec((None, d, TF_FFN), lambda i, j: (layer, 0, j)),
            pl.BlockSpec((None, d, TF_FFN), lambda i, j: (layer, 0, nj + j)),
            pl.BlockSpec((None, TF_FFN, d), lambda i, j: (layer, j, 0)),
        ],
        out_specs=pl.BlockSpec((TM_FFN, d), lambda i, j: (i, 0)),
        out_shape=jax.ShapeDtypeStruct((t, d), F32),
        scratch_shapes=[pltpu.VMEM((TM_FFN, d), BF16), pltpu.VMEM((TM_FFN, LANES), F32)],
        compiler_params=_params(("parallel", "arbitrary")),
        name="ffn",
    )(x2, g.reshape(1, d), scale, shift, gate, w_in, w_in, w_out)


def _rope_lanes(a, axis):
    x1, x2 = jnp.split(a, 2, axis=axis)
    zero = jnp.zeros_like(x1)
    return jnp.concatenate([x1, zero, x2, zero], axis=axis)


def _in_offsets():
    offs, off = [], 0
    for size in IN_SIZES:
        offs.append(off)
        off += size
    return offs


(IN_QM, IN_KM, IN_VM, IN_OM, IN_IM, IN_FM, IN_QLAT, IN_KVLAT, IN_KR, IN_POOL, IN_GATES) = _in_offsets()
IN_W = sum(IN_SIZES)

_Z_COPIES = ((Z_GATES, IN_GATES, N_BRANCH * D_MODEL), (Z_QM, IN_QM, M_QK_W), (Z_QLAT, IN_QLAT, Q_RANK),
             (Z_VM, IN_VM, M_V_W), (Z_OM, IN_OM, M_V_W), (Z_KVLAT, IN_KVLAT, KV_RANK),
             (Z_POOL, IN_POOL, POOL_W))
PREP_COLS = 256
PREP_ROWS = 512


def _w_prep_kernel(w_ref, o_ref, kt_ref, if_ref):
    cols = w_ref.shape[1]
    kt_ref[...] = w_ref[IN_KM:IN_KM + M_QK_W, :].astype(BF16)
    if_ref[...] = jnp.concatenate(
        [w_ref[IN_IM:IN_IM + 2 * M_HEADS, :], jnp.zeros((GATE_ROWS - 2 * M_HEADS, cols), F32)],
        axis=0).astype(BF16)
    for dst, src, width in _Z_COPIES:
        for r in range(0, width, PREP_ROWS):
            n = min(PREP_ROWS, width - r)
            o_ref[dst + r:dst + r + n, :] = w_ref[src + r:src + r + n, :].astype(BF16)
    zero = jnp.zeros((ROPE_HALF, cols), BF16)
    x1 = w_ref[IN_KR:IN_KR + ROPE_HALF, :].astype(BF16)
    x2 = w_ref[IN_KR + ROPE_HALF:IN_KR + ROPE_DIM, :].astype(BF16)
    for k, part in enumerate((x1, zero, x2, zero, x2, zero, x1, zero)):
        o_ref[Z_KR + k * ROPE_HALF:Z_KR + (k + 1) * ROPE_HALF, :] = part


def _prep_w_in(w_in):
    depth, d, _ = w_in.shape
    return pl.pallas_call(
        _w_prep_kernel,
        grid=(depth, d // PREP_COLS),
        in_specs=[pl.BlockSpec((None, IN_W, PREP_COLS), lambda l, c: (l, 0, c))],
        out_specs=[
            pl.BlockSpec((None, Z_W, PREP_COLS), lambda l, c: (l, 0, c)),
            pl.BlockSpec((None, M_QK_W, PREP_COLS), lambda l, c: (l, 0, c)),
            pl.BlockSpec((None, GATE_ROWS, PREP_COLS), lambda l, c: (l, 0, c)),
        ],
        out_shape=[
            jax.ShapeDtypeStruct((depth, Z_W, d), BF16),
            jax.ShapeDtypeStruct((depth, M_QK_W, d), BF16),
            jax.ShapeDtypeStruct((depth, GATE_ROWS, d), BF16),
        ],
        compiler_params=_params(("parallel", "parallel")),
        name="w_prep",
    )(jnp.swapaxes(w_in, 1, 2))


def _prep_head_cols(a):
    return jnp.concatenate([a[..., :NOPE_DIM], _rope_lanes(a[..., NOPE_DIM:], -1)], axis=-1)


def _prep_rope_swapped(a):
    return _rope_lanes(
        jnp.concatenate([a[..., NOPE_DIM + ROPE_HALF:], a[..., NOPE_DIM:NOPE_DIM + ROPE_HALF]],
                        axis=-1), -1)


def _rope_tables(seq):
    pos = jnp.arange(seq, dtype=F32)
    freqs = ROPE_THETA ** (-jnp.arange(0, ROPE_DIM, 2, dtype=F32) / ROPE_DIM)
    ang = pos[:, None] * freqs[None, :]
    cos, sin = jnp.cos(ang), jnp.sin(ang)
    zero = jnp.zeros_like(cos)
    cos_t = jnp.concatenate([cos, zero, cos, zero], axis=1)
    sin_t = jnp.concatenate([-sin, zero, sin, zero], axis=1)
    return cos_t, sin_t


def kernel(x, c, w_ada, b_ada, g_norm1, w_in, b_mgate, g_mnorm, g_qlat, w_uq, g_kvlat, w_ukv,
           g_qn, g_kn, w_pool, b_pool, s_pool, w_branch, w_out, g_norm2, w_ffn_in, w_ffn_out):
    batch, seq, d = x.shape
    depth = w_in.shape[0]
    t = batch * seq

    mod = _ada(c, w_ada, b_ada)
    cos_t, sin_t = _rope_tables(seq)
    x2 = x.reshape(t, d)
    w_pool_b, w_branch_b, w_out_b = w_pool.astype(BF16), w_branch.astype(BF16), w_out.astype(BF16)
    w_ffn_in_b, w_ffn_out_b = w_ffn_in.astype(BF16), w_ffn_out.astype(BF16)
    w_z, w_kt, w_if = _prep_w_in(w_in)

    for l in range(depth):
        shift1, scale1, gate1, shift2, scale2, gate2 = [
            mod[l, :, k * d:(k + 1) * d].reshape(batch, 1, d) for k in range(6)]

        z, kt, gates_t = _inproj(x2, g_norm1[l], scale1, shift1, w_z, l, w_kt, w_if, seq)

        br_a = _mlstm(z.reshape(batch, seq, Z_W), kt, gates_t, b_mgate[l],
                      g_mnorm[l]).reshape(t, M_V_W)

        w_uq_h = w_uq[l].reshape(Q_RANK, MLA_HEADS, QK_DIM)
        w_q = _prep_head_cols(w_uq_h).reshape(Q_RANK, MLA_HEADS * QK_PAD).astype(BF16)
        w_qs = _prep_rope_swapped(w_uq_h).reshape(Q_RANK, MLA_HEADS * LANES).astype(BF16)
        w_kv = w_ukv[l].reshape(KV_RANK, MLA_HEADS, NOPE_DIM + MLA_V)
        w_k = w_kv[..., :NOPE_DIM].reshape(KV_RANK, MLA_HEADS * NOPE_DIM).astype(BF16)
        w_v = w_kv[..., NOPE_DIM:].reshape(KV_RANK, MLA_W).astype(BF16)
        q, k, v = _mla_proj(z, g_qlat[l], g_kvlat[l], w_q, w_qs, w_k, w_v,
                            _prep_head_cols(g_qn[l]).reshape(1, QK_PAD),
                            _prep_rope_swapped(g_qn[l]).reshape(1, LANES),
                            _prep_head_cols(g_kn[l]).reshape(1, QK_PAD),
                            _prep_rope_swapped(g_kn[l]).reshape(1, LANES), cos_t, sin_t, seq)
        br_b = _attention(q, k, v, batch, seq)

        br_c = _pool(z, w_pool_b, l, b_pool[l], s_pool[l], batch, seq)

        x2 = _merge(br_a, br_b, br_c, z, w_branch_b, w_out_b, l, x2, gate1, seq)
        x2 = _ffn(x2, g_norm2[l], scale2, shift2, gate2, w_ffn_in_b, w_ffn_out_b, l, seq)

    return x2.reshape(batch, seq, d)
```

```python
import functools

import jax
import jax.numpy as jnp
from jax import lax
from jax.experimental import pallas as pl
from jax.experimental.pallas import tpu as pltpu

F32 = jnp.float32
BF16 = jnp.bfloat16

D_MODEL = 2048
M_HEADS = 4
M_QK = 128
M_V = 256
M_QK_W = M_HEADS * M_QK
M_V_W = M_HEADS * M_V
M_CHUNK = 128
MLA_HEADS = 8
NOPE_DIM = 128
ROPE_DIM = 64
ROPE_HALF = ROPE_DIM // 2
QK_DIM = NOPE_DIM + ROPE_DIM
MLA_V = 128
MLA_W = MLA_HEADS * MLA_V
Q_RANK = 512
KV_RANK = 256
ROPE_THETA = 10000.0
POOL_WINDOWS = (2, 4, 8, 16)
POOL_GROUPS = 4
POOL_GROUP_W = 256
POOL_W = POOL_GROUPS * POOL_GROUP_W
N_BRANCH = 3
BRANCH_W = 1024
FFN_DIM = ((8 * D_MODEL // 3 + 255) // 256) * 256
NORM_EPS = 1e-6
LOG2_E = 1.4426950408889634
IN_SIZES = (M_QK_W, M_QK_W, M_V_W, M_V_W, M_HEADS, M_HEADS, Q_RANK, KV_RANK, ROPE_DIM, POOL_W,
            N_BRANCH * D_MODEL)

LANES = 128
QK_PAD = 2 * LANES
GATE_ROWS = 16

Z_GATES = 0
Z_QM = Z_GATES + N_BRANCH * D_MODEL
Z_QLAT = Z_QM + M_QK_W
Z_VM = Z_QLAT + Q_RANK
Z_OM = Z_VM + M_V_W
Z_KVLAT = Z_OM + M_V_W
Z_KR = Z_KVLAT + KV_RANK
Z_POOL = Z_KR + 2 * LANES
Z_W = Z_POOL + POOL_W

V7X_VMEM_LIMIT = 52 * 1024 * 1024

TM_IN = 1024
TN_IN = 1536
NORM_ROWS = 64
NORM_UNROLL = 4
TM_MLA = 1024
TQ = 256
TM_MERGE = 256
TM_FFN = 1024
TM_FFN_SUB = 512
TF_FFN = 512
TN_ADA = 1024


def _params(semantics):
    return pltpu.CompilerParams(dimension_semantics=semantics, vmem_limit_bytes=V7X_VMEM_LIMIT)


def _rms(x, width=None):
    ss = jnp.sum(x * x, axis=-1, keepdims=True)
    n = x.shape[-1] if width is None else width
    return x * lax.rsqrt(ss * (1.0 / n) + NORM_EPS)


def _dot(a, b):
    return jnp.dot(a, b, preferred_element_type=F32)


def _dot_nt(a, b):
    return lax.dot_general(a, b, (((1,), (1,)), ((), ())), preferred_element_type=F32)


def _ada_kernel(c_ref, w_ref, b_ref, o_ref):
    c = c_ref[...]
    ca = (c * jax.nn.sigmoid(c)).astype(BF16)
    o_ref[0] = _dot(ca, w_ref[0].astype(BF16)) + b_ref[0]


def _ada(c, w_ada, b_ada):
    depth, d, n = w_ada.shape
    b = c.shape[0]
    return pl.pallas_call(
        _ada_kernel,
        grid=(depth, n // TN_ADA),
        in_specs=[
            pl.BlockSpec((b, d), lambda l, j: (0, 0)),
            pl.BlockSpec((1, d, TN_ADA), lambda l, j: (l, 0, j)),
            pl.BlockSpec((1, 1, TN_ADA), lambda l, j: (l, 0, j)),
        ],
        out_specs=pl.BlockSpec((1, b, TN_ADA), lambda l, j: (l, 0, j)),
        out_shape=jax.ShapeDtypeStruct((depth, b, n), F32),
        compiler_params=_params(("parallel", "parallel")),
        name="adaln",
    )(c, w_ada, b_ada.reshape(depth, 1, n))


def _norm_modulate_into(x_ref, g_ref, sc_ref, sh_ref, h_scr, r_scr, row0, rows, straight_line):
    n_slabs = rows // NORM_ROWS
    d = x_ref.shape[1]

    def slab(r):
        if isinstance(r, int):
            return slice(row0 + r * NORM_ROWS, row0 + (r + 1) * NORM_ROWS)
        return pl.ds(pl.multiple_of(row0 + r * NORM_ROWS, NORM_ROWS), NORM_ROWS)

    def stats(r, carry):
        x = x_ref[slab(r), :]
        ms = jnp.sum(x * x, axis=-1, keepdims=True) * (1.0 / d)
        r_scr[slab(r), :] = jnp.broadcast_to(lax.rsqrt(ms + NORM_EPS), (NORM_ROWS, LANES))
        return carry

    gain = g_ref[...] * (1.0 + sc_ref[0])
    shift = sh_ref[0]

    def apply(r, carry):
        rstd = r_scr[slab(r), :]
        for t in range(d // LANES):
            c0, c1 = t * LANES, (t + 1) * LANES
            x = x_ref[slab(r), c0:c1]
            h_scr[slab(r), c0:c1] = (x * rstd * gain[:, c0:c1] + shift[:, c0:c1]).astype(BF16)
        return carry

    if straight_line:
        for r in range(n_slabs):
            stats(r, 0)
        for r in range(n_slabs):
            apply(r, 0)
    else:
        lax.fori_loop(0, n_slabs, stats, 0, unroll=NORM_UNROLL)
        lax.fori_loop(0, n_slabs, apply, 0)


def _inproj_kernel(x_ref, g_ref, sc_ref, sh_ref, w_ref, wkt_ref, wif_ref, z_ref, kt_ref, gt_ref,
                   h_scr, r_scr):
    j = pl.program_id(1)
    half = x_ref.shape[0] // 2

    def project(rows):
        h = h_scr[rows, :]
        kt_ref[:, rows] = _dot_nt(wkt_ref[...], h).astype(BF16)
        gt_ref[:, rows] = _dot_nt(wif_ref[...], h)
        z_ref[rows, :] = _dot_nt(h, w_ref[...]).astype(BF16)

    @pl.when(j == 0)
    def _():
        _norm_modulate_into(x_ref, g_ref, sc_ref, sh_ref, h_scr, r_scr, 0, half, False)
        _norm_modulate_into(x_ref, g_ref, sc_ref, sh_ref, h_scr, r_scr, half, half, True)
        project(slice(0, half))
        project(slice(half, 2 * half))

    @pl.when(j > 0)
    def _():
        z_ref[...] = _dot_nt(h_scr[...], w_ref[...]).astype(BF16)


def _inproj(x2, g, scale, shift, w_z, layer, w_kt, w_if, seq):
    t, d = x2.shape
    per_b = seq // TM_IN
    vec = pl.BlockSpec((1, 1, d), lambda i, j: (i // per_b, 0, 0))
    return pl.pallas_call(
        _inproj_kernel,
        grid=(t // TM_IN, Z_W // TN_IN),
        in_specs=[
            pl.BlockSpec((TM_IN, d), lambda i, j: (i, 0)),
            pl.BlockSpec((1, d), lambda i, j: (0, 0)),
            vec, vec,
            pl.BlockSpec((None, TN_IN, d), lambda i, j: (layer, j, 0)),
            pl.BlockSpec((None, M_QK_W, d), lambda i, j: (layer, 0, 0)),
            pl.BlockSpec((None, GATE_ROWS, d), lambda i, j: (layer, 0, 0)),
        ],
        out_specs=[
            pl.BlockSpec((TM_IN, TN_IN), lambda i, j: (i, j)),
            pl.BlockSpec((M_QK_W, TM_IN), lambda i, j: (0, i)),
            pl.BlockSpec((GATE_ROWS, TM_IN), lambda i, j: (0, i)),
        ],
        out_shape=[
            jax.ShapeDtypeStruct((t, Z_W), BF16),
            jax.ShapeDtypeStruct((M_QK_W, t), BF16),
            jax.ShapeDtypeStruct((GATE_ROWS, t), F32),
        ],
        scratch_shapes=[pltpu.VMEM((TM_IN, d), BF16), pltpu.VMEM((TM_IN, LANES), F32)],
        compiler_params=_params(("parallel", "arbitrary")),
        name="inproj",
    )(x2, g.reshape(1, d), scale, shift, w_z, w_kt, w_if)


def _mlstm_kernel(bias_ref, q_ref, kt_ref, v_ref, o_ref, g_ref, gn_ref, out_ref,
                  b_scr, i_scr, ct_scr, nr_scr, num_scr, dn_scr):
    L = M_CHUNK
    nc = g_ref.shape[2]
    scale = M_QK ** -0.5

    lane = lax.broadcasted_iota(jnp.int32, (nc, L), 1)
    for h in range(M_HEADS):
        fg = g_ref[M_HEADS + h, 0] + bias_ref[1, h]
        bc = jnp.minimum(fg, 0.0) - jnp.log(1.0 + jnp.exp(-jnp.abs(fg)))
        sh = 1
        while sh < L:
            bc = bc + jnp.where(lane >= sh, pltpu.roll(bc, sh, axis=1), 0.0)
            sh *= 2
        b_scr[h] = bc
        i_scr[h] = g_ref[h, 0] + bias_ref[0, h]
    ct_scr[...] = jnp.zeros_like(ct_scr)
    nr_scr[...] = jnp.zeros_like(nr_scr)

    rows = lax.broadcasted_iota(jnp.int32, (L, L), 0)
    cols = lax.broadcasted_iota(jnp.int32, (L, L), 1)
    eye = rows == cols
    tril = rows >= cols
    ones = jnp.ones((L, LANES), BF16)

    def to_col(row):
        return jnp.sum(jnp.where(eye, jnp.broadcast_to(row, (L, L)), 0.0), axis=1, keepdims=True)

    def step(c, ms):
        r0 = c * L if isinstance(c, int) else pl.multiple_of(c * L, L)
        heads = range(M_HEADS)
        qb = [q_ref[0, pl.ds(r0, L), h * M_QK:(h + 1) * M_QK] for h in heads]
        kt = [kt_ref[h * M_QK:(h + 1) * M_QK, pl.ds(r0, L)] for h in heads]
        vb = [v_ref[0, pl.ds(r0, L), h * M_V:(h + 1) * M_V] for h in heads]
        b_row = [b_scr[h, pl.ds(c, 1), :] for h in heads]
        i_row = [i_scr[h, pl.ds(c, 1), :] for h in heads]
        ct = [ct_scr[h] for h in heads]
        nr = [nr_scr[h] for h in heads]

        qk = [_dot(qb[h], kt[h]) for h in heads]
        inter = [_dot(qb[h], ct[h].astype(BF16)) for h in heads]
        qn = [_dot(qb[h], nr[h].astype(BF16)) for h in heads]

        m_out = []
        for h in heads:
            b_last = b_row[h][:, L - 1:L]
            g_row = b_last - b_row[h] + i_row[h]
            m_new = jnp.maximum(b_last + ms[h], jnp.max(g_row, axis=1, keepdims=True))
            decay = jnp.exp(b_last + ms[h] - m_new)
            ktw = (kt[h].astype(F32) * (jnp.exp(g_row - m_new) * scale)).astype(BF16)
            ct_scr[h] = decay * ct[h] + _dot(ktw, vb[h])
            nr_scr[h] = decay * nr[h] + _dot(ktw, ones)
            m_out.append(m_new)

        w_inter, e_neg, sb = [], [], []
        for h in heads:
            b_rep = jnp.broadcast_to(to_col(b_row[h]), (L, LANES))
            a_rep = b_rep + ms[h]
            dm = jnp.where(tril, b_rep - b_row[h] + i_row[h], -jnp.inf)
            m_t = jnp.maximum(a_rep, jnp.broadcast_to(jnp.max(dm, axis=1, keepdims=True), (L, LANES)))
            w_inter.append(jnp.exp(a_rep - m_t))
            e_neg.append(jnp.exp(-m_t))
            sb.append(((qk[h] * scale) * jnp.exp(dm - m_t)).astype(BF16))

        sv = [_dot(sb[h], vb[h]) for h in heads]
        ssum = [_dot(sb[h], ones) for h in heads]

        for h in heads:
            w_wide = jnp.concatenate([w_inter[h]] * (M_V // LANES), axis=1)
            num_scr[h] = w_wide * inter[h] + sv[h]
            den = w_inter[h] * qn[h] + ssum[h]
            dn_scr[h] = jnp.maximum(jnp.abs(den), e_neg[h])
        return tuple(m_out)

    def finish(c):
        r0 = c * L if isinstance(c, int) else pl.multiple_of(c * L, L)
        for h in range(M_HEADS):
            inv = 1.0 / dn_scr[h]
            hh = num_scr[h] * jnp.concatenate([inv] * (M_V // LANES), axis=1)
            o = o_ref[0, pl.ds(r0, L), h * M_V:(h + 1) * M_V].astype(F32)
            out_ref[0, pl.ds(r0, L), h * M_V:(h + 1) * M_V] = (
                (_rms(hh) * gn_ref[h:h + 1, :]) * jax.nn.sigmoid(o)).astype(BF16)

    def pipelined(c, ms):
        finish(c - 1)
        return step(c, ms)

    ms = step(0, tuple(jnp.zeros((1, 1), F32) for _ in range(M_HEADS)))
    lax.fori_loop(1, nc, pipelined, ms)
    finish(nc - 1)


def _mlstm(z3, kt, gates_t, b_mgate, g_mnorm):
    b, s, _ = z3.shape
    nc = s // M_CHUNK
    g4 = gates_t.reshape(GATE_ROWS, b, nc, M_CHUNK)
    return pl.pallas_call(
        _mlstm_kernel,
        grid=(b,),
        in_specs=[
            pl.BlockSpec(memory_space=pltpu.SMEM),
            pl.BlockSpec((1, s, M_QK_W), lambda i: (i, 0, Z_QM // M_QK_W)),
            pl.BlockSpec((M_QK_W, s), lambda i: (0, i)),
            pl.BlockSpec((1, s, M_V_W), lambda i: (i, 0, Z_VM // M_V_W)),
            pl.BlockSpec((1, s, M_V_W), lambda i: (i, 0, Z_OM // M_V_W)),
            pl.BlockSpec((GATE_ROWS, 1, nc, M_CHUNK), lambda i: (0, i, 0, 0)),
            pl.BlockSpec((M_HEADS, M_V), lambda i: (0, 0)),
        ],
        out_specs=pl.BlockSpec((1, s, M_V_W), lambda i: (i, 0, 0)),
        out_shape=jax.ShapeDtypeStruct((b, s, M_V_W), BF16),
        scratch_shapes=[
            pltpu.VMEM((M_HEADS, nc, M_CHUNK), F32), pltpu.VMEM((M_HEADS, nc, M_CHUNK), F32),
            pltpu.VMEM((M_HEADS, M_QK, M_V), F32), pltpu.VMEM((M_HEADS, M_QK, LANES), F32),
            pltpu.VMEM((M_HEADS, M_CHUNK, M_V), F32), pltpu.VMEM((M_HEADS, M_CHUNK, LANES), F32),
        ],
        compiler_params=_params(("parallel",)),
        name="mlstm",
    )(b_mgate, z3, kt, z3, z3, g4, g_mnorm)


def _mla_proj_kernel(ql_ref, kvl_ref, kr_ref, krs_ref, gq_ref, gkv_ref, wq_ref, wqs_ref, wk_ref,
                     wv_ref, gqn_ref, gqs_ref, gkn_ref, gks_ref, cos_ref, sin_ref,
                     q_out, k_out, v_out):
    scale = QK_DIM ** -0.5 * LOG2_E
    qn = (_rms(ql_ref[...].astype(F32)) * gq_ref[...]).astype(BF16)
    kvn = (_rms(kvl_ref[...].astype(F32)) * gkv_ref[...]).astype(BF16)
    qf = _dot(qn, wq_ref[...])
    qs = _dot(qn, wqs_ref[...])
    kf = _dot(kvn, wk_ref[...])
    v_out[...] = _dot(kvn, wv_ref[...]).astype(BF16)

    cos_t = cos_ref[...]
    sin_t = sin_ref[...]
    gqn = gqn_ref[...] * scale
    gkn = gkn_ref[...]
    q_cos = gqn[:, LANES:] * cos_t
    q_sin = (gqs_ref[...] * scale) * sin_t
    kr = kr_ref[...].astype(F32)
    kr_ss = jnp.sum(kr * kr, axis=-1, keepdims=True)
    k_rot = kr * (gkn[:, LANES:] * cos_t) + krs_ref[...].astype(F32) * (gks_ref[...] * sin_t)
    for h in range(MLA_HEADS):
        qh = qf[:, h * QK_PAD:(h + 1) * QK_PAD]
        rq = lax.rsqrt(jnp.sum(qh * qh, axis=-1, keepdims=True) * (1.0 / QK_DIM) + NORM_EPS)
        q_out[:, h * QK_PAD:h * QK_PAD + LANES] = (qh[:, :LANES] * rq * gqn[:, :LANES]).astype(BF16)
        q_out[:, h * QK_PAD + LANES:(h + 1) * QK_PAD] = (
            (qh[:, LANES:] * q_cos + qs[:, h * LANES:(h + 1) * LANES] * q_sin) * rq).astype(BF16)

        kh = kf[:, h * NOPE_DIM:(h + 1) * NOPE_DIM]
        ss = jnp.sum(kh * kh, axis=-1, keepdims=True) + kr_ss
        rk = lax.rsqrt(ss * (1.0 / QK_DIM) + NORM_EPS)
        k_out[:, h * QK_PAD:h * QK_PAD + LANES] = (kh * rk * gkn[:, :LANES]).astype(BF16)
        k_out[:, h * QK_PAD + LANES:(h + 1) * QK_PAD] = (k_rot * rk).astype(BF16)


def _mla_proj(z, g_qlat, g_kvlat, w_q, w_qs, w_k, w_v, g_qn, g_qs, g_kn, g_ks, cos_t, sin_t, seq):
    t = z.shape[0]
    per_b = seq // TM_MLA
    full = lambda shape: pl.BlockSpec(shape, lambda i: (0,) * len(shape))
    return pl.pallas_call(
        _mla_proj_kernel,
        grid=(t // TM_MLA,),
        in_specs=[
            pl.BlockSpec((TM_MLA, Q_RANK), lambda i: (i, Z_QLAT // Q_RANK)),
            pl.BlockSpec((TM_MLA, KV_RANK), lambda i: (i, Z_KVLAT // KV_RANK)),
            pl.BlockSpec((TM_MLA, LANES), lambda i: (i, Z_KR // LANES)),
            pl.BlockSpec((TM_MLA, LANES), lambda i: (i, Z_KR // LANES + 1)),
            full((1, Q_RANK)), full((1, KV_RANK)),
            full(w_q.shape), full(w_qs.shape), full(w_k.shape), full(w_v.shape),
            full((1, QK_PAD)), full((1, LANES)), full((1, QK_PAD)), full((1, LANES)),
            pl.BlockSpec((TM_MLA, LANES), lambda i: (i % per_b, 0)),
            pl.BlockSpec((TM_MLA, LANES), lambda i: (i % per_b, 0)),
        ],
        out_specs=[
            pl.BlockSpec((TM_MLA, MLA_HEADS * QK_PAD), lambda i: (i, 0)),
            pl.BlockSpec((TM_MLA, MLA_HEADS * QK_PAD), lambda i: (i, 0)),
            pl.BlockSpec((TM_MLA, MLA_W), lambda i: (i, 0)),
        ],
        out_shape=[
            jax.ShapeDtypeStruct((t, MLA_HEADS * QK_PAD), BF16),
            jax.ShapeDtypeStruct((t, MLA_HEADS * QK_PAD), BF16),
            jax.ShapeDtypeStruct((t, MLA_W), BF16),
        ],
        compiler_params=_params(("parallel",)),
        name="mla_proj",
    )(z, z, z, z, g_qlat.reshape(1, Q_RANK), g_kvlat.reshape(1, KV_RANK), w_q, w_qs, w_k, w_v,
      g_qn, g_qs, g_kn, g_ks, cos_t, sin_t)


def _attn_kernel(q_ref, k_ref, v_ref, o_ref):
    s = q_ref.shape[0]
    rows = lax.broadcasted_iota(jnp.int32, (TQ, TQ), 0)
    cols = lax.broadcasted_iota(jnp.int32, (TQ, TQ), 1)
    causal = rows >= cols

    def logits(blk):
        s0, s1 = blk * TQ, (blk + 1) * TQ
        q = q_ref[s0:s1, :]
        diag = _dot_nt(q, k_ref[s0:s1, :])
        past = _dot_nt(q, k_ref[:s0, :]) if blk else None
        return diag, past

    n_blk = s // TQ
    nxt = logits(0)
    for blk in range(n_blk):
        s0, s1 = blk * TQ, (blk + 1) * TQ
        diag, past = nxt
        if blk + 1 < n_blk:
            nxt = logits(blk + 1)
        diag = jnp.where(causal, diag, -jnp.inf)
        m = jnp.max(diag, axis=-1, keepdims=True)
        if blk:
            m = jnp.maximum(m, jnp.max(past, axis=-1, keepdims=True))
        p = jnp.exp2(diag - m)
        denom = jnp.sum(p, axis=-1, keepdims=True)
        acc = _dot(p.astype(BF16), v_ref[s0:s1, :])
        if blk:
            pp = jnp.exp2(past - m)
            denom = denom + jnp.sum(pp, axis=-1, keepdims=True)
            acc = acc + _dot(pp.astype(BF16), v_ref[:s0, :])
        o_ref[s0:s1, :] = (acc * (1.0 / denom)).astype(BF16)


def _attention(q, k, v, batch, seq):
    t = q.shape[0]
    return pl.pallas_call(
        _attn_kernel,
        grid=(batch, MLA_HEADS),
        in_specs=[
            pl.BlockSpec((seq, QK_PAD), lambda b, h: (b, h)),
            pl.BlockSpec((seq, QK_PAD), lambda b, h: (b, h)),
            pl.BlockSpec((seq, MLA_V), lambda b, h: (b, h)),
        ],
        out_specs=pl.BlockSpec((seq, MLA_V), lambda b, h: (b, h)),
        out_shape=jax.ShapeDtypeStruct((t, MLA_W), BF16),
        compiler_params=_params(("parallel", "parallel")),
        name="attention",
    )(q, k, v)


def _pool_kernel(u0_ref, u1_ref, u2_ref, u3_ref, w_ref, b_ref, s_ref, o_ref):
    s = o_ref.shape[0]
    row = lax.broadcasted_iota(jnp.int32, (s, 1), 0)
    for g, (u_ref, win) in enumerate(zip((u0_ref, u1_ref, u2_ref, u3_ref), POOL_WINDOWS)):
        c0, c1 = g * POOL_GROUP_W, (g + 1) * POOL_GROUP_W
        u = u_ref[...].astype(F32)
        acc = u
        sh = 1
        while sh < win:
            acc = acc + jnp.where(row >= sh, pltpu.roll(acc, sh, axis=0), 0.0)
            sh *= 2
        inv_cnt = 1.0 / jnp.minimum(row + 1, win).astype(F32)
        pooled = (acc * inv_cnt - u).astype(BF16)
        y = _dot(pooled, w_ref[g]) + b_ref[:, c0:c1]
        o_ref[:, c0:c1] = (y * s_ref[:, c0:c1]).astype(BF16)


def _pool(z, w_pool, layer, b_pool, s_pool, batch, seq):
    t = z.shape[0]
    group = lambda g: pl.BlockSpec((seq, POOL_GROUP_W), lambda b: (b, Z_POOL // POOL_GROUP_W + g))
    return pl.pallas_call(
        _pool_kernel,
        grid=(batch,),
        in_specs=[
            group(0), group(1), group(2), group(3),
            pl.BlockSpec((None,) + w_pool.shape[1:], lambda b: (layer, 0, 0, 0)),
            pl.BlockSpec((1, POOL_W), lambda b: (0, 0)),
            pl.BlockSpec((1, POOL_W), lambda b: (0, 0)),
        ],
        out_specs=pl.BlockSpec((seq, POOL_W), lambda b: (b, 0)),
        out_shape=jax.ShapeDtypeStruct((t, POOL_W), BF16),
        compiler_params=_params(("parallel",)),
        name="pool",
    )(z, z, z, z, w_pool, b_pool.reshape(1, POOL_W), s_pool.reshape(1, POOL_W))


def _merge_kernel(a_ref, b_ref, c_ref, ga_ref, gb_ref, gc_ref, wb_ref, wo_ref, x_ref, gate_ref,
                  o_ref):
    merged = (jax.nn.sigmoid(ga_ref[...].astype(F32)) * _dot(a_ref[...], wb_ref[0])
              + jax.nn.sigmoid(gb_ref[...].astype(F32)) * _dot(b_ref[...], wb_ref[1])
              + jax.nn.sigmoid(gc_ref[...].astype(F32)) * _dot(c_ref[...], wb_ref[2]))
    o_ref[...] = x_ref[...] + gate_ref[0] * _dot(merged.astype(BF16), wo_ref[...])


def _merge(br_a, br_b, br_c, z, w_branch, w_out, layer, x2, gate1, seq):
    t, d = x2.shape
    per_b = seq // TM_MERGE
    g0 = Z_GATES // d
    br = pl.BlockSpec((TM_MERGE, BRANCH_W), lambda i: (i, 0))
    gate = lambda k: pl.BlockSpec((TM_MERGE, d), lambda i: (i, g0 + k))
    return pl.pallas_call(
        _merge_kernel,
        grid=(t // TM_MERGE,),
        in_specs=[
            br, br, br, gate(0), gate(1), gate(2),
            pl.BlockSpec((None, N_BRANCH, BRANCH_W, d), lambda i: (layer, 0, 0, 0),
                         pipeline_mode=pl.Buffered(1)),
            pl.BlockSpec((None, d, d), lambda i: (layer, 0, 0), pipeline_mode=pl.Buffered(1)),
            pl.BlockSpec((TM_MERGE, d), lambda i: (i, 0)),
            pl.BlockSpec((1, 1, d), lambda i: (i // per_b, 0, 0)),
        ],
        out_specs=pl.BlockSpec((TM_MERGE, d), lambda i: (i, 0)),
        out_shape=jax.ShapeDtypeStruct((t, d), F32),
        compiler_params=_params(("parallel",)),
        name="merge",
    )(br_a, br_b, br_c, z, z, z, w_branch, w_out, x2, gate1)


def _ffn_kernel(x_ref, g_ref, sc_ref, sh_ref, gate_ref, wg_ref, wu_ref, wo_ref, o_ref,
                h_scr, r_scr):
    j = pl.program_id(1)
    last = pl.num_programs(1) - 1

    def hidden_tile(combine):
        for r0 in range(0, TM_FFN, TM_FFN_SUB):
            rows = slice(r0, r0 + TM_FFN_SUB)
            h = h_scr[rows, :]
            gp = _dot(h, wg_ref[...])
            up = _dot(h, wu_ref[...])
            act = ((gp * jax.nn.sigmoid(gp)) * up).astype(BF16)
            combine(rows, _dot(act, wo_ref[...]))

    def write(rows, contrib):
        o_ref[rows, :] = contrib

    def add(rows, contrib):
        o_ref[rows, :] += contrib

    def add_residual(rows, contrib):
        o_ref[rows, :] = x_ref[rows, :] + gate_ref[0] * (o_ref[rows, :] + contrib)

    @pl.when(j == 0)
    def _():
        _norm_modulate_into(x_ref, g_ref, sc_ref, sh_ref, h_scr, r_scr, 0, TM_FFN_SUB, False)
        _norm_modulate_into(x_ref, g_ref, sc_ref, sh_ref, h_scr, r_scr, TM_FFN_SUB,
                            TM_FFN - TM_FFN_SUB, True)
        hidden_tile(write)

    @pl.when(jnp.logical_and(j > 0, j < last))
    def _():
        hidden_tile(add)

    @pl.when(j == last)
    def _():
        hidden_tile(add_residual)


def _ffn(x2, g, scale, shift, gate, w_in, w_out, layer, seq):
    t, d = x2.shape
    per_b = seq // TM_FFN
    nj = FFN_DIM // TF_FFN
    assert nj >= 2, "the first and the last hidden tile must be different grid steps"
    vec = pl.BlockSpec((1, 1, d), lambda i, j: (i // per_b, 0, 0))
    return pl.pallas_call(
        _ffn_kernel,
        grid=(t // TM_FFN, nj),
        in_specs=[
            pl.BlockSpec((TM_FFN, d), lambda i, j: (i, 0)),
            pl.BlockSpec((1, d), lambda i, j: (0, 0)),
            vec, vec, vec,
            pl.BlockSpec((None, d, TF_FFN), lambda i, j: (layer, 0, j)),
            pl.BlockSpec((None, d, TF_FFN), lambda i, j: (layer, 0, nj + j)),
            pl.BlockSpec((None, TF_FFN, d), lambda i, j: (layer, j, 0)),
        ],
        out_specs=pl.BlockSpec((TM_FFN, d), lambda i, j: (i, 0)),
        out_shape=jax.ShapeDtypeStruct((t, d), F32),
        scratch_shapes=[pltpu.VMEM((TM_FFN, d), BF16), pltpu.VMEM((TM_FFN, LANES), F32)],
        compiler_params=_params(("parallel", "arbitrary")),
        name="ffn",
    )(x2, g.reshape(1, d), scale, shift, gate, w_in, w_in, w_out)


def _rope_lanes(a, axis):
    x1, x2 = jnp.split(a, 2, axis=axis)
    zero = jnp.zeros_like(x1)
    return jnp.concatenate([x1, zero, x2, zero], axis=axis)


def _in_offsets():
    offs, off = [], 0
    for size in IN_SIZES:
        offs.append(off)
        off += size
    return offs


(IN_QM, IN_KM, IN_VM, IN_OM, IN_IM, IN_FM, IN_QLAT, IN_KVLAT, IN_KR, IN_POOL, IN_GATES) = _in_offsets()
IN_W = sum(IN_SIZES)

_Z_COPIES = ((Z_GATES, IN_GATES, N_BRANCH * D_MODEL), (Z_QM, IN_QM, M_QK_W), (Z_QLAT, IN_QLAT, Q_RANK),
             (Z_VM, IN_VM, M_V_W), (Z_OM, IN_OM, M_V_W), (Z_KVLAT, IN_KVLAT, KV_RANK),
             (Z_POOL, IN_POOL, POOL_W))
PREP_COLS = 256
PREP_ROWS = 512


def _w_prep_kernel(w_ref, o_ref, kt_ref, if_ref):
    cols = w_ref.shape[1]
    kt_ref[...] = w_ref[IN_KM:IN_KM + M_QK_W, :].astype(BF16)
    if_ref[...] = jnp.concatenate(
        [w_ref[IN_IM:IN_IM + 2 * M_HEADS, :], jnp.zeros((GATE_ROWS - 2 * M_HEADS, cols), F32)],
        axis=0).astype(BF16)
    for dst, src, width in _Z_COPIES:
        for r in range(0, width, PREP_ROWS):
            n = min(PREP_ROWS, width - r)
            o_ref[dst + r:dst + r + n, :] = w_ref[src + r:src + r + n, :].astype(BF16)
    zero = jnp.zeros((ROPE_HALF, cols), BF16)
    x1 = w_ref[IN_KR:IN_KR + ROPE_HALF, :].astype(BF16)
    x2 = w_ref[IN_KR + ROPE_HALF:IN_KR + ROPE_DIM, :].astype(BF16)
    for k, part in enumerate((x1, zero, x2, zero, x2, zero, x1, zero)):
        o_ref[Z_KR + k * ROPE_HALF:Z_KR + (k + 1) * ROPE_HALF, :] = part


def _prep_w_in(w_in):
    depth, d, _ = w_in.shape
    return pl.pallas_call(
        _w_prep_kernel,
        grid=(depth, d // PREP_COLS),
        in_specs=[pl.BlockSpec((None, IN_W, PREP_COLS), lambda l, c: (l, 0, c))],
        out_specs=[
            pl.BlockSpec((None, Z_W, PREP_COLS), lambda l, c: (l, 0, c)),
            pl.BlockSpec((None, M_QK_W, PREP_COLS), lambda l, c: (l, 0, c)),
            pl.BlockSpec((None, GATE_ROWS, PREP_COLS), lambda l, c: (l, 0, c)),
        ],
        out_shape=[
            jax.ShapeDtypeStruct((depth, Z_W, d), BF16),
            jax.ShapeDtypeStruct((depth, M_QK_W, d), BF16),
            jax.ShapeDtypeStruct((depth, GATE_ROWS, d), BF16),
        ],
        compiler_params=_params(("parallel", "parallel")),
        name="w_prep",
    )(jnp.swapaxes(w_in, 1, 2))


def _prep_head_cols(a):
    return jnp.concatenate([a[..., :NOPE_DIM], _rope_lanes(a[..., NOPE_DIM:], -1)], axis=-1)


def _prep_rope_swapped(a):
    return _rope_lanes(
        jnp.concatenate([a[..., NOPE_DIM + ROPE_HALF:], a[..., NOPE_DIM:NOPE_DIM + ROPE_HALF]],
                        axis=-1), -1)


def _rope_tables(seq):
    pos = jnp.arange(seq, dtype=F32)
    freqs = ROPE_THETA ** (-jnp.arange(0, ROPE_DIM, 2, dtype=F32) / ROPE_DIM)
    ang = pos[:, None] * freqs[None, :]
    cos, sin = jnp.cos(ang), jnp.sin(ang)
    zero = jnp.zeros_like(cos)
    cos_t = jnp.concatenate([cos, zero, cos, zero], axis=1)
    sin_t = jnp.concatenate([-sin, zero, sin, zero], axis=1)
    return cos_t, sin_t


def kernel(x, c, w_ada, b_ada, g_norm1, w_in, b_mgate, g_mnorm, g_qlat, w_uq, g_kvlat, w_ukv,
           g_qn, g_kn, w_pool, b_pool, s_pool, w_branch, w_out, g_norm2, w_ffn_in, w_ffn_out):
    batch, seq, d = x.shape
    depth = w_in.shape[0]
    t = batch * seq

    mod = _ada(c, w_ada, b_ada)
    cos_t, sin_t = _rope_tables(seq)
    x2 = x.reshape(t, d)
    w_pool_b, w_branch_b, w_out_b = w_pool.astype(BF16), w_branch.astype(BF16), w_out.astype(BF16)
    w_ffn_in_b, w_ffn_out_b = w_ffn_in.astype(BF16), w_ffn_out.astype(BF16)
    w_z, w_kt, w_if = _prep_w_in(w_in)

    for l in range(depth):
        shift1, scale1, gate1, shift2, scale2, gate2 = [
            mod[l, :, k * d:(k + 1) * d].reshape(batch, 1, d) for k in range(6)]

        z, kt, gates_t = _inproj(x2, g_norm1[l], scale1, shift1, w_z, l, w_kt, w_if, seq)

        br_a = _mlstm(z.reshape(batch, seq, Z_W), kt, gates_t, b_mgate[l],
                      g_mnorm[l]).reshape(t, M_V_W)

        w_uq_h = w_uq[l].reshape(Q_RANK, MLA_HEADS, QK_DIM)
        w_q = _prep_head_cols(w_uq_h).reshape(Q_RANK, MLA_HEADS * QK_PAD).astype(BF16)
        w_qs = _prep_rope_swapped(w_uq_h).reshape(Q_RANK, MLA_HEADS * LANES).astype(BF16)
        w_kv = w_ukv[l].reshape(KV_RANK, MLA_HEADS, NOPE_DIM + MLA_V)
        w_k = w_kv[..., :NOPE_DIM].reshape(KV_RANK, MLA_HEADS * NOPE_DIM).astype(BF16)
        w_v = w_kv[..., NOPE_DIM:].reshape(KV_RANK, MLA_W).astype(BF16)
        q, k, v = _mla_proj(z, g_qlat[l], g_kvlat[l], w_q, w_qs, w_k, w_v,
                            _prep_head_cols(g_qn[l]).reshape(1, QK_PAD),
                            _prep_rope_swapped(g_qn[l]).reshape(1, LANES),
                            _prep_head_cols(g_kn[l]).reshape(1, QK_PAD),
                            _prep_rope_swapped(g_kn[l]).reshape(1, LANES), cos_t, sin_t, seq)
        br_b = _attention(q, k, v, batch, seq)

        br_c = _pool(z, w_pool_b, l, b_pool[l], s_pool[l], batch, seq)

        x2 = _merge(br_a, br_b, br_c, z, w_branch_b, w_out_b, l, x2, gate1, seq)
        x2 = _ffn(x2, g_norm2[l], scale2, shift2, gate2, w_ffn_in_b, w_ffn_out_b, l, seq)

    return x2.reshape(batch, seq, d)
```

```python
import functools

import jax
import jax.numpy as jnp
from jax import lax
from jax.experimental import pallas as pl
from jax.experimental.pallas import tpu as pltpu

F32 = jnp.float32
BF16 = jnp.bfloat16

D_MODEL = 2048
M_HEADS = 4
M_QK = 128
M_V = 256
M_QK_W = M_HEADS * M_QK
M_V_W = M_HEADS * M_V
M_CHUNK = 128
MLA_HEADS = 8
NOPE_DIM = 128
ROPE_DIM = 64
ROPE_HALF = ROPE_DIM // 2
QK_DIM = NOPE_DIM + ROPE_DIM
MLA_V = 128
MLA_W = MLA_HEADS * MLA_V
Q_RANK = 512
KV_RANK = 256
ROPE_THETA = 10000.0
POOL_WINDOWS = (2, 4, 8, 16)
POOL_GROUPS = 4
POOL_GROUP_W = 256
POOL_W = POOL_GROUPS * POOL_GROUP_W
N_BRANCH = 3
BRANCH_W = 1024
FFN_DIM = ((8 * D_MODEL // 3 + 255) // 256) * 256
NORM_EPS = 1e-6
LOG2_E = 1.4426950408889634
IN_SIZES = (M_QK_W, M_QK_W, M_V_W, M_V_W, M_HEADS, M_HEADS, Q_RANK, KV_RANK, ROPE_DIM, POOL_W,
            N_BRANCH * D_MODEL)

LANES = 128
QK_PAD = 2 * LANES
GATE_ROWS = 16

Z_GATES = 0
Z_QM = Z_GATES + N_BRANCH * D_MODEL
Z_QLAT = Z_QM + M_QK_W
Z_VM = Z_QLAT + Q_RANK
Z_OM = Z_VM + M_V_W
Z_KVLAT = Z_OM + M_V_W
Z_KR = Z_KVLAT + KV_RANK
Z_POOL = Z_KR + 2 * LANES
Z_W = Z_POOL + POOL_W

V7X_VMEM_LIMIT = 52 * 1024 * 1024

TM_IN = 1024
TN_IN = 1792
MLSTM_UNROLL = 3
NORM_ROWS = 64
NORM_UNROLL = 4
TM_MLA = 1024
TQ = 256
TM_MERGE = 256
TM_FFN = 1024
TM_FFN_SUB = 512
TF_FFN = 512
TN_ADA = 1024


def _params(semantics):
    return pltpu.CompilerParams(dimension_semantics=semantics, vmem_limit_bytes=V7X_VMEM_LIMIT)


def _rms(x, width=None):
    ss = jnp.sum(x * x, axis=-1, keepdims=True)
    n = x.shape[-1] if width is None else width
    return x * lax.rsqrt(ss * (1.0 / n) + NORM_EPS)


def _dot(a, b):
    return jnp.dot(a, b, preferred_element_type=F32)


def _dot_nt(a, b):
    return lax.dot_general(a, b, (((1,), (1,)), ((), ())), preferred_element_type=F32)


def _ada_kernel(c_ref, w_ref, b_ref, o_ref):
    c = c_ref[...]
    ca = (c * jax.nn.sigmoid(c)).astype(BF16)
    o_ref[0] = _dot(ca, w_ref[0].astype(BF16)) + b_ref[0]


def _ada(c, w_ada, b_ada):
    depth, d, n = w_ada.shape
    b = c.shape[0]
    return pl.pallas_call(
        _ada_kernel,
        grid=(depth, n // TN_ADA),
        in_specs=[
            pl.BlockSpec((b, d), lambda l, j: (0, 0)),
            pl.BlockSpec((1, d, TN_ADA), lambda l, j: (l, 0, j)),
            pl.BlockSpec((1, 1, TN_ADA), lambda l, j: (l, 0, j)),
        ],
        out_specs=pl.BlockSpec((1, b, TN_ADA), lambda l, j: (l, 0, j)),
        out_shape=jax.ShapeDtypeStruct((depth, b, n), F32),
        compiler_params=_params(("parallel", "parallel")),
        name="adaln",
    )(c, w_ada, b_ada.reshape(depth, 1, n))


def _norm_modulate_into(x_ref, g_ref, sc_ref, sh_ref, h_scr, r_scr, row0, rows, straight_line):
    n_slabs = rows // NORM_ROWS
    d = x_ref.shape[1]

    def slab(r):
        if isinstance(r, int):
            return slice(row0 + r * NORM_ROWS, row0 + (r + 1) * NORM_ROWS)
        return pl.ds(pl.multiple_of(row0 + r * NORM_ROWS, NORM_ROWS), NORM_ROWS)

    def stats(r, carry):
        x = x_ref[slab(r), :]
        ms = jnp.sum(x * x, axis=-1, keepdims=True) * (1.0 / d)
        r_scr[slab(r), :] = jnp.broadcast_to(lax.rsqrt(ms + NORM_EPS), (NORM_ROWS, LANES))
        return carry

    gain = g_ref[...] * (1.0 + sc_ref[0])
    shift = sh_ref[0]

    def apply(r, carry):
        rstd = r_scr[slab(r), :]
        for t in range(d // LANES):
            c0, c1 = t * LANES, (t + 1) * LANES
            x = x_ref[slab(r), c0:c1]
            h_scr[slab(r), c0:c1] = (x * rstd * gain[:, c0:c1] + shift[:, c0:c1]).astype(BF16)
        return carry

    if straight_line:
        for r in range(n_slabs):
            stats(r, 0)
        for r in range(n_slabs):
            apply(r, 0)
    else:
        lax.fori_loop(0, n_slabs, stats, 0, unroll=NORM_UNROLL)
        lax.fori_loop(0, n_slabs, apply, 0)


def _inproj_kernel(x_ref, g_ref, sc_ref, sh_ref, w_ref, wkt_ref, wif_ref, z_ref, kt_ref, gt_ref,
                   h_scr, r_scr):
    j = pl.program_id(1)
    half = x_ref.shape[0] // 2

    def project(rows):
        h = h_scr[rows, :]
        kt_ref[:, rows] = _dot_nt(wkt_ref[...], h).astype(BF16)
        gt_ref[:, rows] = _dot_nt(wif_ref[...], h)
        z_ref[rows, :] = _dot_nt(h, w_ref[...]).astype(BF16)

    @pl.when(j == 0)
    def _():
        _norm_modulate_into(x_ref, g_ref, sc_ref, sh_ref, h_scr, r_scr, 0, half, False)
        _norm_modulate_into(x_ref, g_ref, sc_ref, sh_ref, h_scr, r_scr, half, half, True)
        project(slice(0, half))
        project(slice(half, 2 * half))

    @pl.when(j > 0)
    def _():
        z_ref[...] = _dot_nt(h_scr[...], w_ref[...]).astype(BF16)


def _inproj(x2, g, scale, shift, w_z, layer, w_kt, w_if, seq):
    t, d = x2.shape
    per_b = seq // TM_IN
    vec = pl.BlockSpec((1, 1, d), lambda i, j: (i // per_b, 0, 0))
    return pl.pallas_call(
        _inproj_kernel,
        grid=(t // TM_IN, Z_W // TN_IN),
        in_specs=[
            pl.BlockSpec((TM_IN, d), lambda i, j: (i, 0)),
            pl.BlockSpec((1, d), lambda i, j: (0, 0)),
            vec, vec,
            pl.BlockSpec((None, TN_IN, d), lambda i, j: (layer, j, 0)),
            pl.BlockSpec((None, M_QK_W, d), lambda i, j: (layer, 0, 0)),
            pl.BlockSpec((None, GATE_ROWS, d), lambda i, j: (layer, 0, 0)),
        ],
        out_specs=[
            pl.BlockSpec((TM_IN, TN_IN), lambda i, j: (i, j)),
            pl.BlockSpec((M_QK_W, TM_IN), lambda i, j: (0, i)),
            pl.BlockSpec((GATE_ROWS, TM_IN), lambda i, j: (0, i)),
        ],
        out_shape=[
            jax.ShapeDtypeStruct((t, Z_W), BF16),
            jax.ShapeDtypeStruct((M_QK_W, t), BF16),
            jax.ShapeDtypeStruct((GATE_ROWS, t), F32),
        ],
        scratch_shapes=[pltpu.VMEM((TM_IN, d), BF16), pltpu.VMEM((TM_IN, LANES), F32)],
        compiler_params=_params(("parallel", "arbitrary")),
        name="inproj",
    )(x2, g.reshape(1, d), scale, shift, w_z, w_kt, w_if)


def _mlstm_kernel(bias_ref, q_ref, kt_ref, v_ref, o_ref, g_ref, gn_ref, out_ref,
                  b_scr, i_scr, ct_scr, nr_scr, num_scr, dn_scr):
    L = M_CHUNK
    nc = g_ref.shape[2]
    scale = M_QK ** -0.5

    lane = lax.broadcasted_iota(jnp.int32, (nc, L), 1)
    for h in range(M_HEADS):
        fg = g_ref[M_HEADS + h, 0] + bias_ref[1, h]
        bc = jnp.minimum(fg, 0.0) - jnp.log(1.0 + jnp.exp(-jnp.abs(fg)))
        sh = 1
        while sh < L:
            bc = bc + jnp.where(lane >= sh, pltpu.roll(bc, sh, axis=1), 0.0)
            sh *= 2
        b_scr[h] = bc
        i_scr[h] = g_ref[h, 0] + bias_ref[0, h]
    ct_scr[...] = jnp.zeros_like(ct_scr)
    nr_scr[...] = jnp.zeros_like(nr_scr)

    rows = lax.broadcasted_iota(jnp.int32, (L, L), 0)
    cols = lax.broadcasted_iota(jnp.int32, (L, L), 1)
    eye = rows == cols
    tril = rows >= cols
    ones = jnp.ones((L, LANES), BF16)

    def to_col(row):
        return jnp.sum(jnp.where(eye, jnp.broadcast_to(row, (L, L)), 0.0), axis=1, keepdims=True)

    def step(c, ms):
        r0 = c * L if isinstance(c, int) else pl.multiple_of(c * L, L)
        heads = range(M_HEADS)
        qb = [q_ref[0, pl.ds(r0, L), h * M_QK:(h + 1) * M_QK] for h in heads]
        kt = [kt_ref[h * M_QK:(h + 1) * M_QK, pl.ds(r0, L)] for h in heads]
        vb = [v_ref[0, pl.ds(r0, L), h * M_V:(h + 1) * M_V] for h in heads]
        b_row = [b_scr[h, pl.ds(c, 1), :] for h in heads]
        i_row = [i_scr[h, pl.ds(c, 1), :] for h in heads]
        ct = [ct_scr[h] for h in heads]
        nr = [nr_scr[h] for h in heads]

        qk = [_dot(qb[h], kt[h]) for h in heads]
        inter = [_dot(qb[h], ct[h].astype(BF16)) for h in heads]
        qn = [_dot(qb[h], nr[h].astype(BF16)) for h in heads]

        m_out = []
        for h in heads:
            b_last = b_row[h][:, L - 1:L]
            g_row = b_last - b_row[h] + i_row[h]
            m_new = jnp.maximum(b_last + ms[h], jnp.max(g_row, axis=1, keepdims=True))
            decay = jnp.exp(b_last + ms[h] - m_new)
            ktw = (kt[h].astype(F32) * (jnp.exp(g_row - m_new) * scale)).astype(BF16)
            ct_scr[h] = decay * ct[h] + _dot(ktw, vb[h])
            nr_scr[h] = decay * nr[h] + _dot(ktw, ones)
            m_out.append(m_new)

        w_inter, e_neg, sb = [], [], []
        for h in heads:
            b_rep = jnp.broadcast_to(to_col(b_row[h]), (L, LANES))
            a_rep = b_rep + ms[h]
            dm = jnp.where(tril, b_rep - b_row[h] + i_row[h], -jnp.inf)
            m_t = jnp.maximum(a_rep, jnp.broadcast_to(jnp.max(dm, axis=1, keepdims=True), (L, LANES)))
            w_inter.append(jnp.exp(a_rep - m_t))
            e_neg.append(jnp.exp(-m_t))
            sb.append(((qk[h] * scale) * jnp.exp(dm - m_t)).astype(BF16))

        sv = [_dot(sb[h], vb[h]) for h in heads]
        ssum = [_dot(sb[h], ones) for h in heads]

        for h in heads:
            w_wide = jnp.concatenate([w_inter[h]] * (M_V // LANES), axis=1)
            num_scr[h] = w_wide * inter[h] + sv[h]
            den = w_inter[h] * qn[h] + ssum[h]
            dn_scr[h] = jnp.maximum(jnp.abs(den), e_neg[h])
        return tuple(m_out)

    def finish(c):
        r0 = c * L if isinstance(c, int) else pl.multiple_of(c * L, L)
        for h in range(M_HEADS):
            inv = 1.0 / dn_scr[h]
            hh = num_scr[h] * jnp.concatenate([inv] * (M_V // LANES), axis=1)
            o = o_ref[0, pl.ds(r0, L), h * M_V:(h + 1) * M_V].astype(F32)
            out_ref[0, pl.ds(r0, L), h * M_V:(h + 1) * M_V] = (
                (_rms(hh) * gn_ref[h:h + 1, :]) * jax.nn.sigmoid(o)).astype(BF16)

    def pipelined(c, ms):
        finish(c - 1)
        return step(c, ms)

    ms = step(0, tuple(jnp.zeros((1, 1), F32) for _ in range(M_HEADS)))
    lax.fori_loop(1, nc, pipelined, ms, unroll=MLSTM_UNROLL)
    finish(nc - 1)


def _mlstm(z3, kt, gates_t, b_mgate, g_mnorm):
    b, s, _ = z3.shape
    nc = s // M_CHUNK
    g4 = gates_t.reshape(GATE_ROWS, b, nc, M_CHUNK)
    return pl.pallas_call(
        _mlstm_kernel,
        grid=(b,),
        in_specs=[
            pl.BlockSpec(memory_space=pltpu.SMEM),
            pl.BlockSpec((1, s, M_QK_W), lambda i: (i, 0, Z_QM // M_QK_W)),
            pl.BlockSpec((M_QK_W, s), lambda i: (0, i)),
            pl.BlockSpec((1, s, M_V_W), lambda i: (i, 0, Z_VM // M_V_W)),
            pl.BlockSpec((1, s, M_V_W), lambda i: (i, 0, Z_OM // M_V_W)),
            pl.BlockSpec((GATE_ROWS, 1, nc, M_CHUNK), lambda i: (0, i, 0, 0)),
            pl.BlockSpec((M_HEADS, M_V), lambda i: (0, 0)),
        ],
        out_specs=pl.BlockSpec((1, s, M_V_W), lambda i: (i, 0, 0)),
        out_shape=jax.ShapeDtypeStruct((b, s, M_V_W), BF16),
        scratch_shapes=[
            pltpu.VMEM((M_HEADS, nc, M_CHUNK), F32), pltpu.VMEM((M_HEADS, nc, M_CHUNK), F32),
            pltpu.VMEM((M_HEADS, M_QK, M_V), F32), pltpu.VMEM((M_HEADS, M_QK, LANES), F32),
            pltpu.VMEM((M_HEADS, M_CHUNK, M_V), F32), pltpu.VMEM((M_HEADS, M_CHUNK, LANES), F32),
        ],
        compiler_params=_params(("parallel",)),
        name="mlstm",
    )(b_mgate, z3, kt, z3, z3, g4, g_mnorm)


def _mla_proj_kernel(ql_ref, kvl_ref, kr_ref, krs_ref, gq_ref, gkv_ref, wq_ref, wqs_ref, wk_ref,
                     wv_ref, gqn_ref, gqs_ref, gkn_ref, gks_ref, cos_ref, sin_ref,
                     q_out, k_out, v_out):
    scale = QK_DIM ** -0.5 * LOG2_E
    qn = (_rms(ql_ref[...].astype(F32)) * gq_ref[...]).astype(BF16)
    kvn = (_rms(kvl_ref[...].astype(F32)) * gkv_ref[...]).astype(BF16)
    qf = _dot(qn, wq_ref[...])
    qs = _dot(qn, wqs_ref[...])
    kf = _dot(kvn, wk_ref[...])
    v_out[...] = _dot(kvn, wv_ref[...]).astype(BF16)

    cos_t = cos_ref[...]
    sin_t = sin_ref[...]
    gqn = gqn_ref[...] * scale
    gkn = gkn_ref[...]
    q_cos = gqn[:, LANES:] * cos_t
    q_sin = (gqs_ref[...] * scale) * sin_t
    kr = kr_ref[...].astype(F32)
    kr_ss = jnp.sum(kr * kr, axis=-1, keepdims=True)
    k_rot = kr * (gkn[:, LANES:] * cos_t) + krs_ref[...].astype(F32) * (gks_ref[...] * sin_t)
    for h in range(MLA_HEADS):
        qh = qf[:, h * QK_PAD:(h + 1) * QK_PAD]
        rq = lax.rsqrt(jnp.sum(qh * qh, axis=-1, keepdims=True) * (1.0 / QK_DIM) + NORM_EPS)
        q_out[:, h * QK_PAD:h * QK_PAD + LANES] = (qh[:, :LANES] * rq * gqn[:, :LANES]).astype(BF16)
        q_out[:, h * QK_PAD + LANES:(h + 1) * QK_PAD] = (
            (qh[:, LANES:] * q_cos + qs[:, h * LANES:(h + 1) * LANES] * q_sin) * rq).astype(BF16)

        kh = kf[:, h * NOPE_DIM:(h + 1) * NOPE_DIM]
        ss = jnp.sum(kh * kh, axis=-1, keepdims=True) + kr_ss
        rk = lax.rsqrt(ss * (1.0 / QK_DIM) + NORM_EPS)
        k_out[:, h * QK_PAD:h * QK_PAD + LANES] = (kh * rk * gkn[:, :LANES]).astype(BF16)
        k_out[:, h * QK_PAD + LANES:(h + 1) * QK_PAD] = (k_rot * rk).astype(BF16)


def _mla_proj(z, g_qlat, g_kvlat, w_q, w_qs, w_k, w_v, g_qn, g_qs, g_kn, g_ks, cos_t, sin_t, seq):
    t = z.shape[0]
    per_b = seq // TM_MLA
    full = lambda shape: pl.BlockSpec(shape, lambda i: (0,) * len(shape))
    return pl.pallas_call(
        _mla_proj_kernel,
        grid=(t // TM_MLA,),
        in_specs=[
            pl.BlockSpec((TM_MLA, Q_RANK), lambda i: (i, Z_QLAT // Q_RANK)),
            pl.BlockSpec((TM_MLA, KV_RANK), lambda i: (i, Z_KVLAT // KV_RANK)),
            pl.BlockSpec((TM_MLA, LANES), lambda i: (i, Z_KR // LANES)),
            pl.BlockSpec((TM_MLA, LANES), lambda i: (i, Z_KR // LANES + 1)),
            full((1, Q_RANK)), full((1, KV_RANK)),
            full(w_q.shape), full(w_qs.shape), full(w_k.shape), full(w_v.shape),
            full((1, QK_PAD)), full((1, LANES)), full((1, QK_PAD)), full((1, LANES)),
            pl.BlockSpec((TM_MLA, LANES), lambda i: (i % per_b, 0)),
            pl.BlockSpec((TM_MLA, LANES), lambda i: (i % per_b, 0)),
        ],
        out_specs=[
            pl.BlockSpec((TM_MLA, MLA_HEADS * QK_PAD), lambda i: (i, 0)),
            pl.BlockSpec((TM_MLA, MLA_HEADS * QK_PAD), lambda i: (i, 0)),
            pl.BlockSpec((TM_MLA, MLA_W), lambda i: (i, 0)),
        ],
        out_shape=[
            jax.ShapeDtypeStruct((t, MLA_HEADS * QK_PAD), BF16),
            jax.ShapeDtypeStruct((t, MLA_HEADS * QK_PAD), BF16),
            jax.ShapeDtypeStruct((t, MLA_W), BF16),
        ],
        compiler_params=_params(("parallel",)),
        name="mla_proj",
    )(z, z, z, z, g_qlat.reshape(1, Q_RANK), g_kvlat.reshape(1, KV_RANK), w_q, w_qs, w_k, w_v,
      g_qn, g_qs, g_kn, g_ks, cos_t, sin_t)


def _attn_kernel(q_ref, k_ref, v_ref, o_ref):
    s = q_ref.shape[0]
    rows = lax.broadcasted_iota(jnp.int32, (TQ, TQ), 0)
    cols = lax.broadcasted_iota(jnp.int32, (TQ, TQ), 1)
    causal = rows >= cols

    def logits(blk):
        s0, s1 = blk * TQ, (blk + 1) * TQ
        q = q_ref[s0:s1, :]
        diag = _dot_nt(q, k_ref[s0:s1, :])
        past = _dot_nt(q, k_ref[:s0, :]) if blk else None
        return diag, past

    n_blk = s // TQ
    nxt = logits(0)
    for blk in range(n_blk):
        s0, s1 = blk * TQ, (blk + 1) * TQ
        diag, past = nxt
        if blk + 1 < n_blk:
            nxt = logits(blk + 1)
        diag = jnp.where(causal, diag, -jnp.inf)
        m = jnp.max(diag, axis=-1, keepdims=True)
        if blk:
            m = jnp.maximum(m, jnp.max(past, axis=-1, keepdims=True))
        p = jnp.exp2(diag - m)
        denom = jnp.sum(p, axis=-1, keepdims=True)
        acc = _dot(p.astype(BF16), v_ref[s0:s1, :])
        if blk:
            pp = jnp.exp2(past - m)
            denom = denom + jnp.sum(pp, axis=-1, keepdims=True)
            acc = acc + _dot(pp.astype(BF16), v_ref[:s0, :])
        o_ref[s0:s1, :] = (acc * (1.0 / denom)).astype(BF16)


def _attention(q, k, v, batch, seq):
    t = q.shape[0]
    return pl.pallas_call(
        _attn_kernel,
        grid=(batch, MLA_HEADS),
        in_specs=[
            pl.BlockSpec((seq, QK_PAD), lambda b, h: (b, h)),
            pl.BlockSpec((seq, QK_PAD), lambda b, h: (b, h)),
            pl.BlockSpec((seq, MLA_V), lambda b, h: (b, h)),
        ],
        out_specs=pl.BlockSpec((seq, MLA_V), lambda b, h: (b, h)),
        out_shape=jax.ShapeDtypeStruct((t, MLA_W), BF16),
        compiler_params=_params(("parallel", "parallel")),
        name="attention",
    )(q, k, v)


def _pool_kernel(u0_ref, u1_ref, u2_ref, u3_ref, w_ref, b_ref, s_ref, o_ref):
    s = o_ref.shape[0]
    row = lax.broadcasted_iota(jnp.int32, (s, 1), 0)
    for g, (u_ref, win) in enumerate(zip((u0_ref, u1_ref, u2_ref, u3_ref), POOL_WINDOWS)):
        c0, c1 = g * POOL_GROUP_W, (g + 1) * POOL_GROUP_W
        u = u_ref[...].astype(F32)
        acc = u
        sh = 1
        while sh < win:
            acc = acc + jnp.where(row >= sh, pltpu.roll(acc, sh, axis=0), 0.0)
            sh *= 2
        inv_cnt = 1.0 / jnp.minimum(row + 1, win).astype(F32)
        pooled = (acc * inv_cnt - u).astype(BF16)
        y = _dot(pooled, w_ref[g]) + b_ref[:, c0:c1]
        o_ref[:, c0:c1] = (y * s_ref[:, c0:c1]).astype(BF16)


def _pool(z, w_pool, layer, b_pool, s_pool, batch, seq):
    t = z.shape[0]
    group = lambda g: pl.BlockSpec((seq, POOL_GROUP_W), lambda b: (b, Z_POOL // POOL_GROUP_W + g))
    return pl.pallas_call(
        _pool_kernel,
        grid=(batch,),
        in_specs=[
            group(0), group(1), group(2), group(3),
            pl.BlockSpec((None,) + w_pool.shape[1:], lambda b: (layer, 0, 0, 0)),
            pl.BlockSpec((1, POOL_W), lambda b: (0, 0)),
            pl.BlockSpec((1, POOL_W), lambda b: (0, 0)),
        ],
        out_specs=pl.BlockSpec((seq, POOL_W), lambda b: (b, 0)),
        out_shape=jax.ShapeDtypeStruct((t, POOL_W), BF16),
        compiler_params=_params(("parallel",)),
        name="pool",
    )(z, z, z, z, w_pool, b_pool.reshape(1, POOL_W), s_pool.reshape(1, POOL_W))


def _merge_kernel(a_ref, b_ref, c_ref, ga_ref, gb_ref, gc_ref, wb_ref, wo_ref, x_ref, gate_ref,
                  o_ref):
    merged = (jax.nn.sigmoid(ga_ref[...].astype(F32)) * _dot(a_ref[...], wb_ref[0])
              + jax.nn.sigmoid(gb_ref[...].astype(F32)) * _dot(b_ref[...], wb_ref[1])
              + jax.nn.sigmoid(gc_ref[...].astype(F32)) * _dot(c_ref[...], wb_ref[2]))
    o_ref[...] = x_ref[...] + gate_ref[0] * _dot(merged.astype(BF16), wo_ref[...])


def _merge(br_a, br_b, br_c, z, w_branch, w_out, layer, x2, gate1, seq):
    t, d = x2.shape
    per_b = seq // TM_MERGE
    g0 = Z_GATES // d
    br = pl.BlockSpec((TM_MERGE, BRANCH_W), lambda i: (i, 0))
    gate = lambda k: pl.BlockSpec((TM_MERGE, d), lambda i: (i, g0 + k))
    return pl.pallas_call(
        _merge_kernel,
        grid=(t // TM_MERGE,),
        in_specs=[
            br, br, br, gate(0), gate(1), gate(2),
            pl.BlockSpec((None, N_BRANCH, BRANCH_W, d), lambda i: (layer, 0, 0, 0),
                         pipeline_mode=pl.Buffered(1)),
            pl.BlockSpec((None, d, d), lambda i: (layer, 0, 0), pipeline_mode=pl.Buffered(1)),
            pl.BlockSpec((TM_MERGE, d), lambda i: (i, 0)),
            pl.BlockSpec((1, 1, d), lambda i: (i // per_b, 0, 0)),
        ],
        out_specs=pl.BlockSpec((TM_MERGE, d), lambda i: (i, 0)),
        out_shape=jax.ShapeDtypeStruct((t, d), F32),
        compiler_params=_params(("parallel",)),
        name="merge",
    )(br_a, br_b, br_c, z, z, z, w_branch, w_out, x2, gate1)


def _ffn_kernel(x_ref, g_ref, sc_ref, sh_ref, gate_ref, wg_ref, wu_ref, wo_ref, o_ref,
                h_scr, r_scr):
    j = pl.program_id(1)
    last = pl.num_programs(1) - 1

    def hidden_tile(combine):
        for r0 in range(0, TM_FFN, TM_FFN_SUB):
            rows = slice(r0, r0 + TM_FFN_SUB)
            h = h_scr[rows, :]
            gp = _dot(h, wg_ref[...])
            up = _dot(h, wu_ref[...])
            act = ((gp * jax.nn.sigmoid(gp)) * up).astype(BF16)
            combine(rows, _dot(act, wo_ref[...]))

    def write(rows, contrib):
        o_ref[rows, :] = contrib

    def add(rows, contrib):
        o_ref[rows, :] += contrib

    def add_residual(rows, contrib):
        o_ref[rows, :] = x_ref[rows, :] + gate_ref[0] * (o_ref[rows, :] + contrib)

    @pl.when(j == 0)
    def _():
        _norm_modulate_into(x_ref, g_ref, sc_ref, sh_ref, h_scr, r_scr, 0, TM_FFN_SUB, False)
        _norm_modulate_into(x_ref, g_ref, sc_ref, sh_ref, h_scr, r_scr, TM_FFN_SUB,
                            TM_FFN - TM_FFN_SUB, True)
        hidden_tile(write)

    @pl.when(jnp.logical_and(j > 0, j < last))
    def _():
        hidden_tile(add)

    @pl.when(j == last)
    def _():
        hidden_tile(add_residual)


def _ffn(x2, g, scale, shift, gate, w_in, w_out, layer, seq):
    t, d = x2.shape
    per_b = seq // TM_FFN
    nj = FFN_DIM // TF_FFN
    assert nj >= 2, "the first and the last hidden tile must be different grid steps"
    vec = pl.BlockSpec((1, 1, d), lambda i, j: (i // per_b, 0, 0))
    return pl.pallas_call(
        _ffn_kernel,
        grid=(t // TM_FFN, nj),
        in_specs=[
            pl.BlockSpec((TM_FFN, d), lambda i, j: (i, 0)),
            pl.BlockSpec((1, d), lambda i, j: (0, 0)),
            vec, vec, vec,
            pl.BlockSpec((None, d, TF_FFN), lambda i, j: (layer, 0, j)),
            pl.BlockSpec((None, d, TF_FFN), lambda i, j: (layer, 0, nj + j)),
            pl.BlockSpec((None, TF_FFN, d), lambda i, j: (layer, j, 0)),
        ],
        out_specs=pl.BlockSpec((TM_FFN, d), lambda i, j: (i, 0)),
        out_shape=jax.ShapeDtypeStruct((t, d), F32),
        scratch_shapes=[pltpu.VMEM((TM_FFN, d), BF16), pltpu.VMEM((TM_FFN, LANES), F32)],
        compiler_params=_params(("parallel", "arbitrary")),
        name="ffn",
    )(x2, g.reshape(1, d), scale, shift, gate, w_in, w_in, w_out)


def _rope_lanes(a, axis):
    x1, x2 = jnp.split(a, 2, axis=axis)
    zero = jnp.zeros_like(x1)
    return jnp.concatenate([x1, zero, x2, zero], axis=axis)


def _in_offsets():
    offs, off = [], 0
    for size in IN_SIZES:
        offs.append(off)
        off += size
    return offs


(IN_QM, IN_KM, IN_VM, IN_OM, IN_IM, IN_FM, IN_QLAT, IN_KVLAT, IN_KR, IN_POOL, IN_GATES) = _in_offsets()
IN_W = sum(IN_SIZES)

_Z_COPIES = ((Z_GATES, IN_GATES, N_BRANCH * D_MODEL), (Z_QM, IN_QM, M_QK_W), (Z_QLAT, IN_QLAT, Q_RANK),
             (Z_VM, IN_VM, M_V_W), (Z_OM, IN_OM, M_V_W), (Z_KVLAT, IN_KVLAT, KV_RANK),
             (Z_POOL, IN_POOL, POOL_W))
PREP_COLS = 256
PREP_ROWS = 512


def _w_prep_kernel(w_ref, o_ref, kt_ref, if_ref):
    cols = w_ref.shape[1]
    kt_ref[...] = w_ref[IN_KM:IN_KM + M_QK_W, :].astype(BF16)
    if_ref[...] = jnp.concatenate(
        [w_ref[IN_IM:IN_IM + 2 * M_HEADS, :], jnp.zeros((GATE_ROWS - 2 * M_HEADS, cols), F32)],
        axis=0).astype(BF16)
    for dst, src, width in _Z_COPIES:
        for r in range(0, width, PREP_ROWS):
            n = min(PREP_ROWS, width - r)
            o_ref[dst + r:dst + r + n, :] = w_ref[src + r:src + r + n, :].astype(BF16)
    zero = jnp.zeros((ROPE_HALF, cols), BF16)
    x1 = w_ref[IN_KR:IN_KR + ROPE_HALF, :].astype(BF16)
    x2 = w_ref[IN_KR + ROPE_HALF:IN_KR + ROPE_DIM, :].astype(BF16)
    for k, part in enumerate((x1, zero, x2, zero, x2, zero, x1, zero)):
        o_ref[Z_KR + k * ROPE_HALF:Z_KR + (k + 1) * ROPE_HALF, :] = part


def _prep_w_in(w_in):
    depth, d, _ = w_in.shape
    return pl.pallas_call(
        _w_prep_kernel,
        grid=(depth, d // PREP_COLS),
        in_specs=[pl.BlockSpec((None, IN_W, PREP_COLS), lambda l, c: (l, 0, c))],
        out_specs=[
            pl.BlockSpec((None, Z_W, PREP_COLS), lambda l, c: (l, 0, c)),
            pl.BlockSpec((None, M_QK_W, PREP_COLS), lambda l, c: (l, 0, c)),
            pl.BlockSpec((None, GATE_ROWS, PREP_COLS), lambda l, c: (l, 0, c)),
        ],
        out_shape=[
            jax.ShapeDtypeStruct((depth, Z_W, d), BF16),
            jax.ShapeDtypeStruct((depth, M_QK_W, d), BF16),
            jax.ShapeDtypeStruct((depth, GATE_ROWS, d), BF16),
        ],
        compiler_params=_params(("parallel", "parallel")),
        name="w_prep",
    )(jnp.swapaxes(w_in, 1, 2))


def _prep_head_cols(a):
    return jnp.concatenate([a[..., :NOPE_DIM], _rope_lanes(a[..., NOPE_DIM:], -1)], axis=-1)


def _prep_rope_swapped(a):
    return _rope_lanes(
        jnp.concatenate([a[..., NOPE_DIM + ROPE_HALF:], a[..., NOPE_DIM:NOPE_DIM + ROPE_HALF]],
                        axis=-1), -1)


def _rope_tables(seq):
    pos = jnp.arange(seq, dtype=F32)
    freqs = ROPE_THETA ** (-jnp.arange(0, ROPE_DIM, 2, dtype=F32) / ROPE_DIM)
    ang = pos[:, None] * freqs[None, :]
    cos, sin = jnp.cos(ang), jnp.sin(ang)
    zero = jnp.zeros_like(cos)
    cos_t = jnp.concatenate([cos, zero, cos, zero], axis=1)
    sin_t = jnp.concatenate([-sin, zero, sin, zero], axis=1)
    return cos_t, sin_t


def kernel(x, c, w_ada, b_ada, g_norm1, w_in, b_mgate, g_mnorm, g_qlat, w_uq, g_kvlat, w_ukv,
           g_qn, g_kn, w_pool, b_pool, s_pool, w_branch, w_out, g_norm2, w_ffn_in, w_ffn_out):
    batch, seq, d = x.shape
    depth = w_in.shape[0]
    t = batch * seq

    mod = _ada(c, w_ada, b_ada)
    cos_t, sin_t = _rope_tables(seq)
    x2 = x.reshape(t, d)
    w_pool_b, w_branch_b, w_out_b = w_pool.astype(BF16), w_branch.astype(BF16), w_out.astype(BF16)
    w_ffn_in_b, w_ffn_out_b = w_ffn_in.astype(BF16), w_ffn_out.astype(BF16)
    w_z, w_kt, w_if = _prep_w_in(w_in)

    for l in range(depth):
        shift1, scale1, gate1, shift2, scale2, gate2 = [
            mod[l, :, k * d:(k + 1) * d].reshape(batch, 1, d) for k in range(6)]

        z, kt, gates_t = _inproj(x2, g_norm1[l], scale1, shift1, w_z, l, w_kt, w_if, seq)

        br_a = _mlstm(z.reshape(batch, seq, Z_W), kt, gates_t, b_mgate[l],
                      g_mnorm[l]).reshape(t, M_V_W)

        w_uq_h = w_uq[l].reshape(Q_RANK, MLA_HEADS, QK_DIM)
        w_q = _prep_head_cols(w_uq_h).reshape(Q_RANK, MLA_HEADS * QK_PAD).astype(BF16)
        w_qs = _prep_rope_swapped(w_uq_h).reshape(Q_RANK, MLA_HEADS * LANES).astype(BF16)
        w_kv = w_ukv[l].reshape(KV_RANK, MLA_HEADS, NOPE_DIM + MLA_V)
        w_k = w_kv[..., :NOPE_DIM].reshape(KV_RANK, MLA_HEADS * NOPE_DIM).astype(BF16)
        w_v = w_kv[..., NOPE_DIM:].reshape(KV_RANK, MLA_W).astype(BF16)
        q, k, v = _mla_proj(z, g_qlat[l], g_kvlat[l], w_q, w_qs, w_k, w_v,
                            _prep_head_cols(g_qn[l]).reshape(1, QK_PAD),
                            _prep_rope_swapped(g_qn[l]).reshape(1, LANES),
                            _prep_head_cols(g_kn[l]).reshape(1, QK_PAD),
                            _prep_rope_swapped(g_kn[l]).reshape(1, LANES), cos_t, sin_t, seq)
        br_b = _attention(q, k, v, batch, seq)

        br_c = _pool(z, w_pool_b, l, b_pool[l], s_pool[l], batch, seq)

        x2 = _merge(br_a, br_b, br_c, z, w_branch_b, w_out_b, l, x2, gate1, seq)
        x2 = _ffn(x2, g_norm2[l], scale2, shift2, gate2, w_ffn_in_b, w_ffn_out_b, l, seq)

    return x2.reshape(batch, seq, d)
```

```python
import functools

import jax
import jax.numpy as jnp
from jax import lax
from jax.experimental import pallas as pl
from jax.experimental.pallas import tpu as pltpu

F32 = jnp.float32
BF16 = jnp.bfloat16

D_MODEL = 2048
M_HEADS = 4
M_QK = 128
M_V = 256
M_QK_W = M_HEADS * M_QK
M_V_W = M_HEADS * M_V
M_CHUNK = 128
MLA_HEADS = 8
NOPE_DIM = 128
ROPE_DIM = 64
ROPE_HALF = ROPE_DIM // 2
QK_DIM = NOPE_DIM + ROPE_DIM
MLA_V = 128
MLA_W = MLA_HEADS * MLA_V
Q_RANK = 512
KV_RANK = 256
ROPE_THETA = 10000.0
POOL_WINDOWS = (2, 4, 8, 16)
POOL_GROUPS = 4
POOL_GROUP_W = 256
POOL_W = POOL_GROUPS * POOL_GROUP_W
N_BRANCH = 3
BRANCH_W = 1024
FFN_DIM = ((8 * D_MODEL // 3 + 255) // 256) * 256
NORM_EPS = 1e-6
LOG2_E = 1.4426950408889634
IN_SIZES = (M_QK_W, M_QK_W, M_V_W, M_V_W, M_HEADS, M_HEADS, Q_RANK, KV_RANK, ROPE_DIM, POOL_W,
            N_BRANCH * D_MODEL)

LANES = 128
QK_PAD = 2 * LANES
V_PAD = 2 * MLA_V
GATE_ROWS = 16

Z_GATES = 0
Z_QM = Z_GATES + N_BRANCH * D_MODEL
Z_QLAT = Z_QM + M_QK_W
Z_VM = Z_QLAT + Q_RANK
Z_OM = Z_VM + M_V_W
Z_KVLAT = Z_OM + M_V_W
Z_KR = Z_KVLAT + KV_RANK
Z_POOL = Z_KR + 2 * LANES
Z_W = Z_POOL + POOL_W

V7X_VMEM_LIMIT = 52 * 1024 * 1024

TM_IN = 1024
TN_IN = 1792
MLSTM_UNROLL = 3
NORM_ROWS = 64
NORM_UNROLL = 4
TM_MLA = 1024
TQ = 256
TM_MERGE = 256
TM_FFN = 1024
TM_FFN_SUB = 512
TF_FFN = 512
TN_ADA = 1024


def _params(semantics):
    return pltpu.CompilerParams(dimension_semantics=semantics, vmem_limit_bytes=V7X_VMEM_LIMIT)


def _rms(x, width=None):
    ss = jnp.sum(x * x, axis=-1, keepdims=True)
    n = x.shape[-1] if width is None else width
    return x * lax.rsqrt(ss * (1.0 / n) + NORM_EPS)


def _dot(a, b):
    return jnp.dot(a, b, preferred_element_type=F32)


def _dot_nt(a, b):
    return lax.dot_general(a, b, (((1,), (1,)), ((), ())), preferred_element_type=F32)


def _ada_kernel(c_ref, w_ref, b_ref, o_ref):
    c = c_ref[...]
    ca = (c * jax.nn.sigmoid(c)).astype(BF16)
    o_ref[0] = _dot(ca, w_ref[0].astype(BF16)) + b_ref[0]


def _ada(c, w_ada, b_ada):
    depth, d, n = w_ada.shape
    b = c.shape[0]
    return pl.pallas_call(
        _ada_kernel,
        grid=(depth, n // TN_ADA),
        in_specs=[
            pl.BlockSpec((b, d), lambda l, j: (0, 0)),
            pl.BlockSpec((1, d, TN_ADA), lambda l, j: (l, 0, j)),
            pl.BlockSpec((1, 1, TN_ADA), lambda l, j: (l, 0, j)),
        ],
        out_specs=pl.BlockSpec((1, b, TN_ADA), lambda l, j: (l, 0, j)),
        out_shape=jax.ShapeDtypeStruct((depth, b, n), F32),
        compiler_params=_params(("parallel", "parallel")),
        name="adaln",
    )(c, w_ada, b_ada.reshape(depth, 1, n))


def _norm_modulate_into(x_ref, g_ref, sc_ref, sh_ref, h_scr, r_scr, row0, rows, straight_line):
    n_slabs = rows // NORM_ROWS
    d = x_ref.shape[1]

    def slab(r):
        if isinstance(r, int):
            return slice(row0 + r * NORM_ROWS, row0 + (r + 1) * NORM_ROWS)
        return pl.ds(pl.multiple_of(row0 + r * NORM_ROWS, NORM_ROWS), NORM_ROWS)

    def stats(r, carry):
        x = x_ref[slab(r), :]
        ms = jnp.sum(x * x, axis=-1, keepdims=True) * (1.0 / d)
        r_scr[slab(r), :] = jnp.broadcast_to(lax.rsqrt(ms + NORM_EPS), (NORM_ROWS, LANES))
        return carry

    gain = g_ref[...] * (1.0 + sc_ref[0])
    shift = sh_ref[0]

    def apply(r, carry):
        rstd = r_scr[slab(r), :]
        for t in range(d // LANES):
            c0, c1 = t * LANES, (t + 1) * LANES
            x = x_ref[slab(r), c0:c1]
            h_scr[slab(r), c0:c1] = (x * rstd * gain[:, c0:c1] + shift[:, c0:c1]).astype(BF16)
        return carry

    if straight_line:
        for r in range(n_slabs):
            stats(r, 0)
        for r in range(n_slabs):
            apply(r, 0)
    else:
        lax.fori_loop(0, n_slabs, stats, 0, unroll=NORM_UNROLL)
        lax.fori_loop(0, n_slabs, apply, 0)


def _inproj_kernel(x_ref, g_ref, sc_ref, sh_ref, w_ref, wkt_ref, wif_ref, z_ref, kt_ref, gt_ref,
                   h_scr, r_scr):
    j = pl.program_id(1)
    half = x_ref.shape[0] // 2

    def project(rows):
        h = h_scr[rows, :]
        kt_ref[:, rows] = _dot_nt(wkt_ref[...], h).astype(BF16)
        gt_ref[:, rows] = _dot_nt(wif_ref[...], h)
        z_ref[rows, :] = _dot_nt(h, w_ref[...]).astype(BF16)

    @pl.when(j == 0)
    def _():
        _norm_modulate_into(x_ref, g_ref, sc_ref, sh_ref, h_scr, r_scr, 0, half, False)
        _norm_modulate_into(x_ref, g_ref, sc_ref, sh_ref, h_scr, r_scr, half, half, True)
        project(slice(0, half))
        project(slice(half, 2 * half))

    @pl.when(j > 0)
    def _():
        z_ref[...] = _dot_nt(h_scr[...], w_ref[...]).astype(BF16)


def _inproj(x2, g, scale, shift, w_z, layer, w_kt, w_if, seq):
    t, d = x2.shape
    per_b = seq // TM_IN
    vec = pl.BlockSpec((1, 1, d), lambda i, j: (i // per_b, 0, 0))
    return pl.pallas_call(
        _inproj_kernel,
        grid=(t // TM_IN, Z_W // TN_IN),
        in_specs=[
            pl.BlockSpec((TM_IN, d), lambda i, j: (i, 0)),
            pl.BlockSpec((1, d), lambda i, j: (0, 0)),
            vec, vec,
            pl.BlockSpec((None, TN_IN, d), lambda i, j: (layer, j, 0)),
            pl.BlockSpec((None, M_QK_W, d), lambda i, j: (layer, 0, 0)),
            pl.BlockSpec((None, GATE_ROWS, d), lambda i, j: (layer, 0, 0)),
        ],
        out_specs=[
            pl.BlockSpec((TM_IN, TN_IN), lambda i, j: (i, j)),
            pl.BlockSpec((M_QK_W, TM_IN), lambda i, j: (0, i)),
            pl.BlockSpec((GATE_ROWS, TM_IN), lambda i, j: (0, i)),
        ],
        out_shape=[
            jax.ShapeDtypeStruct((t, Z_W), BF16),
            jax.ShapeDtypeStruct((M_QK_W, t), BF16),
            jax.ShapeDtypeStruct((GATE_ROWS, t), F32),
        ],
        scratch_shapes=[pltpu.VMEM((TM_IN, d), BF16), pltpu.VMEM((TM_IN, LANES), F32)],
        compiler_params=_params(("parallel", "arbitrary")),
        name="inproj",
    )(x2, g.reshape(1, d), scale, shift, w_z, w_kt, w_if)


def _mlstm_kernel(bias_ref, q_ref, kt_ref, v_ref, o_ref, g_ref, gn_ref, out_ref,
                  b_scr, i_scr, ct_scr, nr_scr, num_scr, dn_scr):
    L = M_CHUNK
    nc = g_ref.shape[2]
    scale = M_QK ** -0.5

    lane = lax.broadcasted_iota(jnp.int32, (nc, L), 1)
    for h in range(M_HEADS):
        fg = g_ref[M_HEADS + h, 0] + bias_ref[1, h]
        bc = jnp.minimum(fg, 0.0) - jnp.log(1.0 + jnp.exp(-jnp.abs(fg)))
        sh = 1
        while sh < L:
            bc = bc + jnp.where(lane >= sh, pltpu.roll(bc, sh, axis=1), 0.0)
            sh *= 2
        b_scr[h] = bc
        i_scr[h] = g_ref[h, 0] + bias_ref[0, h]
    ct_scr[...] = jnp.zeros_like(ct_scr)
    nr_scr[...] = jnp.zeros_like(nr_scr)

    rows = lax.broadcasted_iota(jnp.int32, (L, L), 0)
    cols = lax.broadcasted_iota(jnp.int32, (L, L), 1)
    eye = rows == cols
    tril = rows >= cols
    ones = jnp.ones((L, LANES), BF16)

    def to_col(row):
        return jnp.sum(jnp.where(eye, jnp.broadcast_to(row, (L, L)), 0.0), axis=1, keepdims=True)

    def step(c, ms):
        r0 = c * L if isinstance(c, int) else pl.multiple_of(c * L, L)
        heads = range(M_HEADS)
        qb = [q_ref[0, pl.ds(r0, L), h * M_QK:(h + 1) * M_QK] for h in heads]
        kt = [kt_ref[h * M_QK:(h + 1) * M_QK, pl.ds(r0, L)] for h in heads]
        vb = [v_ref[0, pl.ds(r0, L), h * M_V:(h + 1) * M_V] for h in heads]
        b_row = [b_scr[h, pl.ds(c, 1), :] for h in heads]
        i_row = [i_scr[h, pl.ds(c, 1), :] for h in heads]
        ct = [ct_scr[h] for h in heads]
        nr = [nr_scr[h] for h in heads]

        qk = [_dot(qb[h], kt[h]) for h in heads]
        inter = [_dot(qb[h], ct[h].astype(BF16)) for h in heads]
        qn = [_dot(qb[h], nr[h].astype(BF16)) for h in heads]

        m_out = []
        for h in heads:
            b_last = b_row[h][:, L - 1:L]
            g_row = b_last - b_row[h] + i_row[h]
            m_new = jnp.maximum(b_last + ms[h], jnp.max(g_row, axis=1, keepdims=True))
            decay = jnp.exp(b_last + ms[h] - m_new)
            ktw = (kt[h].astype(F32) * (jnp.exp(g_row - m_new) * scale)).astype(BF16)
            ct_scr[h] = decay * ct[h] + _dot(ktw, vb[h])
            nr_scr[h] = decay * nr[h] + _dot(ktw, ones)
            m_out.append(m_new)

        w_inter, e_neg, sb = [], [], []
        for h in heads:
            b_rep = jnp.broadcast_to(to_col(b_row[h]), (L, LANES))
            a_rep = b_rep + ms[h]
            dm = jnp.where(tril, b_rep - b_row[h] + i_row[h], -jnp.inf)
            m_t = jnp.maximum(a_rep, jnp.broadcast_to(jnp.max(dm, axis=1, keepdims=True), (L, LANES)))
            w_inter.append(jnp.exp(a_rep - m_t))
            e_neg.append(jnp.exp(-m_t))
            sb.append(((qk[h] * scale) * jnp.exp(dm - m_t)).astype(BF16))

        sv = [_dot(sb[h], vb[h]) for h in heads]
        ssum = [_dot(sb[h], ones) for h in heads]

        for h in heads:
            w_wide = jnp.concatenate([w_inter[h]] * (M_V // LANES), axis=1)
            num_scr[h] = w_wide * inter[h] + sv[h]
            den = w_inter[h] * qn[h] + ssum[h]
            dn_scr[h] = jnp.maximum(jnp.abs(den), e_neg[h])
        return tuple(m_out)

    def finish(c):
        r0 = c * L if isinstance(c, int) else pl.multiple_of(c * L, L)
        for h in range(M_HEADS):
            inv = 1.0 / dn_scr[h]
            hh = num_scr[h] * jnp.concatenate([inv] * (M_V // LANES), axis=1)
            o = o_ref[0, pl.ds(r0, L), h * M_V:(h + 1) * M_V].astype(F32)
            out_ref[0, pl.ds(r0, L), h * M_V:(h + 1) * M_V] = (
                (_rms(hh) * gn_ref[h:h + 1, :]) * jax.nn.sigmoid(o)).astype(BF16)

    def pipelined(c, ms):
        finish(c - 1)
        return step(c, ms)

    ms = step(0, tuple(jnp.zeros((1, 1), F32) for _ in range(M_HEADS)))
    lax.fori_loop(1, nc, pipelined, ms, unroll=MLSTM_UNROLL)
    finish(nc - 1)


def _mlstm(z3, kt, gates_t, b_mgate, g_mnorm):
    b, s, _ = z3.shape
    nc = s // M_CHUNK
    g4 = gates_t.reshape(GATE_ROWS, b, nc, M_CHUNK)
    return pl.pallas_call(
        _mlstm_kernel,
        grid=(b,),
        in_specs=[
            pl.BlockSpec(memory_space=pltpu.SMEM),
            pl.BlockSpec((1, s, M_QK_W), lambda i: (i, 0, Z_QM // M_QK_W)),
            pl.BlockSpec((M_QK_W, s), lambda i: (0, i)),
            pl.BlockSpec((1, s, M_V_W), lambda i: (i, 0, Z_VM // M_V_W)),
            pl.BlockSpec((1, s, M_V_W), lambda i: (i, 0, Z_OM // M_V_W)),
            pl.BlockSpec((GATE_ROWS, 1, nc, M_CHUNK), lambda i: (0, i, 0, 0)),
            pl.BlockSpec((M_HEADS, M_V), lambda i: (0, 0)),
        ],
        out_specs=pl.BlockSpec((1, s, M_V_W), lambda i: (i, 0, 0)),
        out_shape=jax.ShapeDtypeStruct((b, s, M_V_W), BF16),
        scratch_shapes=[
            pltpu.VMEM((M_HEADS, nc, M_CHUNK), F32), pltpu.VMEM((M_HEADS, nc, M_CHUNK), F32),
            pltpu.VMEM((M_HEADS, M_QK, M_V), F32), pltpu.VMEM((M_HEADS, M_QK, LANES), F32),
            pltpu.VMEM((M_HEADS, M_CHUNK, M_V), F32), pltpu.VMEM((M_HEADS, M_CHUNK, LANES), F32),
        ],
        compiler_params=_params(("parallel",)),
        name="mlstm",
    )(b_mgate, z3, kt, z3, z3, g4, g_mnorm)


def _mla_proj_kernel(ql_ref, kvl_ref, kr_ref, krs_ref, gq_ref, gkv_ref, wq_ref, wqs_ref, wk_ref,
                     wv_ref, gqn_ref, gqs_ref, gkn_ref, gks_ref, cos_ref, sin_ref,
                     q_out, k_out, v_out):
    scale = QK_DIM ** -0.5 * LOG2_E
    qn = (_rms(ql_ref[...].astype(F32)) * gq_ref[...]).astype(BF16)
    kvn = (_rms(kvl_ref[...].astype(F32)) * gkv_ref[...]).astype(BF16)
    qf = _dot(qn, wq_ref[...])
    qs = _dot(qn, wqs_ref[...])
    kf = _dot(kvn, wk_ref[...])
    vf = _dot(kvn, wv_ref[...])
    ones = jnp.ones((vf.shape[0], MLA_V), BF16)
    for h in range(MLA_HEADS):
        v_out[:, h * V_PAD:h * V_PAD + MLA_V] = vf[:, h * MLA_V:(h + 1) * MLA_V].astype(BF16)
        v_out[:, h * V_PAD + MLA_V:(h + 1) * V_PAD] = ones

    cos_t = cos_ref[...]
    sin_t = sin_ref[...]
    gqn = gqn_ref[...] * scale
    gkn = gkn_ref[...]
    q_cos = gqn[:, LANES:] * cos_t
    q_sin = (gqs_ref[...] * scale) * sin_t
    kr = kr_ref[...].astype(F32)
    kr_ss = jnp.sum(kr * kr, axis=-1, keepdims=True)
    k_rot = kr * (gkn[:, LANES:] * cos_t) + krs_ref[...].astype(F32) * (gks_ref[...] * sin_t)
    for h in range(MLA_HEADS):
        qh = qf[:, h * QK_PAD:(h + 1) * QK_PAD]
        rq = lax.rsqrt(jnp.sum(qh * qh, axis=-1, keepdims=True) * (1.0 / QK_DIM) + NORM_EPS)
        q_out[:, h * QK_PAD:h * QK_PAD + LANES] = (qh[:, :LANES] * rq * gqn[:, :LANES]).astype(BF16)
        q_out[:, h * QK_PAD + LANES:(h + 1) * QK_PAD] = (
            (qh[:, LANES:] * q_cos + qs[:, h * LANES:(h + 1) * LANES] * q_sin) * rq).astype(BF16)

        kh = kf[:, h * NOPE_DIM:(h + 1) * NOPE_DIM]
        ss = jnp.sum(kh * kh, axis=-1, keepdims=True) + kr_ss
        rk = lax.rsqrt(ss * (1.0 / QK_DIM) + NORM_EPS)
        k_out[:, h * QK_PAD:h * QK_PAD + LANES] = (kh * rk * gkn[:, :LANES]).astype(BF16)
        k_out[:, h * QK_PAD + LANES:(h + 1) * QK_PAD] = (k_rot * rk).astype(BF16)


def _mla_proj(z, g_qlat, g_kvlat, w_q, w_qs, w_k, w_v, g_qn, g_qs, g_kn, g_ks, cos_t, sin_t, seq):
    t = z.shape[0]
    per_b = seq // TM_MLA
    full = lambda shape: pl.BlockSpec(shape, lambda i: (0,) * len(shape))
    return pl.pallas_call(
        _mla_proj_kernel,
        grid=(t // TM_MLA,),
        in_specs=[
            pl.BlockSpec((TM_MLA, Q_RANK), lambda i: (i, Z_QLAT // Q_RANK)),
            pl.BlockSpec((TM_MLA, KV_RANK), lambda i: (i, Z_KVLAT // KV_RANK)),
            pl.BlockSpec((TM_MLA, LANES), lambda i: (i, Z_KR // LANES)),
            pl.BlockSpec((TM_MLA, LANES), lambda i: (i, Z_KR // LANES + 1)),
            full((1, Q_RANK)), full((1, KV_RANK)),
            full(w_q.shape), full(w_qs.shape), full(w_k.shape), full(w_v.shape),
            full((1, QK_PAD)), full((1, LANES)), full((1, QK_PAD)), full((1, LANES)),
            pl.BlockSpec((TM_MLA, LANES), lambda i: (i % per_b, 0)),
            pl.BlockSpec((TM_MLA, LANES), lambda i: (i % per_b, 0)),
        ],
        out_specs=[
            pl.BlockSpec((TM_MLA, MLA_HEADS * QK_PAD), lambda i: (i, 0)),
            pl.BlockSpec((TM_MLA, MLA_HEADS * QK_PAD), lambda i: (i, 0)),
            pl.BlockSpec((TM_MLA, MLA_HEADS * V_PAD), lambda i: (i, 0)),
        ],
        out_shape=[
            jax.ShapeDtypeStruct((t, MLA_HEADS * QK_PAD), BF16),
            jax.ShapeDtypeStruct((t, MLA_HEADS * QK_PAD), BF16),
            jax.ShapeDtypeStruct((t, MLA_HEADS * V_PAD), BF16),
        ],
        compiler_params=_params(("parallel",)),
        name="mla_proj",
    )(z, z, z, z, g_qlat.reshape(1, Q_RANK), g_kvlat.reshape(1, KV_RANK), w_q, w_qs, w_k, w_v,
      g_qn, g_qs, g_kn, g_ks, cos_t, sin_t)


def _attn_kernel(q_ref, k_ref, v_ref, o_ref):
    s = q_ref.shape[0]
    rows = lax.broadcasted_iota(jnp.int32, (TQ, TQ), 0)
    cols = lax.broadcasted_iota(jnp.int32, (TQ, TQ), 1)
    causal = rows >= cols

    def logits(blk):
        s0, s1 = blk * TQ, (blk + 1) * TQ
        q = q_ref[s0:s1, :]
        diag = _dot_nt(q, k_ref[s0:s1, :])
        past = _dot_nt(q, k_ref[:s0, :]) if blk else None
        return diag, past

    n_blk = s // TQ
    nxt = logits(0)
    for blk in range(n_blk):
        s0, s1 = blk * TQ, (blk + 1) * TQ
        diag, past = nxt
        if blk + 1 < n_blk:
            nxt = logits(blk + 1)
        diag = jnp.where(causal, diag, -jnp.inf)
        m = jnp.max(diag, axis=-1, keepdims=True)
        if blk:
            m = jnp.maximum(m, jnp.max(past, axis=-1, keepdims=True))
        p = jnp.exp2(diag - m)
        acc = _dot(p.astype(BF16), v_ref[s0:s1, :])
        if blk:
            pp = jnp.exp2(past - m)
            acc = acc + _dot(pp.astype(BF16), v_ref[:s0, :])
        o_ref[s0:s1, :] = (acc[:, :MLA_V] * (1.0 / acc[:, MLA_V:])).astype(BF16)


def _attention(q, k, v, batch, seq):
    t = q.shape[0]
    return pl.pallas_call(
        _attn_kernel,
        grid=(batch, MLA_HEADS),
        in_specs=[
            pl.BlockSpec((seq, QK_PAD), lambda b, h: (b, h)),
            pl.BlockSpec((seq, QK_PAD), lambda b, h: (b, h)),
            pl.BlockSpec((seq, V_PAD), lambda b, h: (b, h)),
        ],
        out_specs=pl.BlockSpec((seq, MLA_V), lambda b, h: (b, h)),
        out_shape=jax.ShapeDtypeStruct((t, MLA_W), BF16),
        compiler_params=_params(("parallel", "parallel")),
        name="attention",
    )(q, k, v)


def _pool_kernel(u0_ref, u1_ref, u2_ref, u3_ref, w_ref, b_ref, s_ref, o_ref):
    s = o_ref.shape[0]
    row = lax.broadcasted_iota(jnp.int32, (s, 1), 0)
    for g, (u_ref, win) in enumerate(zip((u0_ref, u1_ref, u2_ref, u3_ref), POOL_WINDOWS)):
        c0, c1 = g * POOL_GROUP_W, (g + 1) * POOL_GROUP_W
        u = u_ref[...].astype(F32)
        acc = u
        sh = 1
        while sh < win:
            acc = acc + jnp.where(row >= sh, pltpu.roll(acc, sh, axis=0), 0.0)
            sh *= 2
        inv_cnt = 1.0 / jnp.minimum(row + 1, win).astype(F32)
        pooled = (acc * inv_cnt - u).astype(BF16)
        y = _dot(pooled, w_ref[g]) + b_ref[:, c0:c1]
        o_ref[:, c0:c1] = (y * s_ref[:, c0:c1]).astype(BF16)


def _pool(z, w_pool, layer, b_pool, s_pool, batch, seq):
    t = z.shape[0]
    group = lambda g: pl.BlockSpec((seq, POOL_GROUP_W), lambda b: (b, Z_POOL // POOL_GROUP_W + g))
    return pl.pallas_call(
        _pool_kernel,
        grid=(batch,),
        in_specs=[
            group(0), group(1), group(2), group(3),
            pl.BlockSpec((None,) + w_pool.shape[1:], lambda b: (layer, 0, 0, 0)),
            pl.BlockSpec((1, POOL_W), lambda b: (0, 0)),
            pl.BlockSpec((1, POOL_W), lambda b: (0, 0)),
        ],
        out_specs=pl.BlockSpec((seq, POOL_W), lambda b: (b, 0)),
        out_shape=jax.ShapeDtypeStruct((t, POOL_W), BF16),
        compiler_params=_params(("parallel",)),
        name="pool",
    )(z, z, z, z, w_pool, b_pool.reshape(1, POOL_W), s_pool.reshape(1, POOL_W))


def _merge_kernel(a_ref, b_ref, c_ref, ga_ref, gb_ref, gc_ref, wb_ref, wo_ref, x_ref, gate_ref,
                  o_ref):
    merged = (jax.nn.sigmoid(ga_ref[...].astype(F32)) * _dot(a_ref[...], wb_ref[0])
              + jax.nn.sigmoid(gb_ref[...].astype(F32)) * _dot(b_ref[...], wb_ref[1])
              + jax.nn.sigmoid(gc_ref[...].astype(F32)) * _dot(c_ref[...], wb_ref[2]))
    o_ref[...] = x_ref[...] + gate_ref[0] * _dot(merged.astype(BF16), wo_ref[...])


def _merge(br_a, br_b, br_c, z, w_branch, w_out, layer, x2, gate1, seq):
    t, d = x2.shape
    per_b = seq // TM_MERGE
    g0 = Z_GATES // d
    br = pl.BlockSpec((TM_MERGE, BRANCH_W), lambda i: (i, 0))
    gate = lambda k: pl.BlockSpec((TM_MERGE, d), lambda i: (i, g0 + k))
    return pl.pallas_call(
        _merge_kernel,
        grid=(t // TM_MERGE,),
        in_specs=[
            br, br, br, gate(0), gate(1), gate(2),
            pl.BlockSpec((None, N_BRANCH, BRANCH_W, d), lambda i: (layer, 0, 0, 0),
                         pipeline_mode=pl.Buffered(1)),
            pl.BlockSpec((None, d, d), lambda i: (layer, 0, 0), pipeline_mode=pl.Buffered(1)),
            pl.BlockSpec((TM_MERGE, d), lambda i: (i, 0)),
            pl.BlockSpec((1, 1, d), lambda i: (i // per_b, 0, 0)),
        ],
        out_specs=pl.BlockSpec((TM_MERGE, d), lambda i: (i, 0)),
        out_shape=jax.ShapeDtypeStruct((t, d), F32),
        compiler_params=_params(("parallel",)),
        name="merge",
    )(br_a, br_b, br_c, z, z, z, w_branch, w_out, x2, gate1)


def _ffn_kernel(x_ref, g_ref, sc_ref, sh_ref, gate_ref, wg_ref, wu_ref, wo_ref, o_ref,
                h_scr, r_scr):
    j = pl.program_id(1)
    last = pl.num_programs(1) - 1

    def hidden_tile(combine):
        for r0 in range(0, TM_FFN, TM_FFN_SUB):
            rows = slice(r0, r0 + TM_FFN_SUB)
            h = h_scr[rows, :]
            gp = _dot(h, wg_ref[...])
            up = _dot(h, wu_ref[...])
            act = ((gp * jax.nn.sigmoid(gp)) * up).astype(BF16)
            combine(rows, _dot(act, wo_ref[...]))

    def write(rows, contrib):
        o_ref[rows, :] = contrib

    def add(rows, contrib):
        o_ref[rows, :] += contrib

    def add_residual(rows, contrib):
        o_ref[rows, :] = x_ref[rows, :] + gate_ref[0] * (o_ref[rows, :] + contrib)

    @pl.when(j == 0)
    def _():
        _norm_modulate_into(x_ref, g_ref, sc_ref, sh_ref, h_scr, r_scr, 0, TM_FFN_SUB, False)
        _norm_modulate_into(x_ref, g_ref, sc_ref, sh_ref, h_scr, r_scr, TM_FFN_SUB,
                            TM_FFN - TM_FFN_SUB, True)
        hidden_tile(write)

    @pl.when(jnp.logical_and(j > 0, j < last))
    def _():
        hidden_tile(add)

    @pl.when(j == last)
    def _():
        hidden_tile(add_residual)


def _ffn(x2, g, scale, shift, gate, w_in, w_out, layer, seq):
    t, d = x2.shape
    per_b = seq // TM_FFN
    nj = FFN_DIM // TF_FFN
    assert nj >= 2, "the first and the last hidden tile must be different grid steps"
    vec = pl.BlockSpec((1, 1, d), lambda i, j: (i // per_b, 0, 0))
    return pl.pallas_call(
        _ffn_kernel,
        grid=(t // TM_FFN, nj),
        in_specs=[
            pl.BlockSpec((TM_FFN, d), lambda i, j: (i, 0)),
            pl.BlockSpec((1, d), lambda i, j: (0, 0)),
            vec, vec, vec,
            pl.BlockSpec((None, d, TF_FFN), lambda i, j: (layer, 0, j)),
            pl.BlockSpec((None, d, TF_FFN), lambda i, j: (layer, 0, nj + j)),
            pl.BlockSpec((None, TF_FFN, d), lambda i, j: (layer, j, 0)),
        ],
        out_specs=pl.BlockSpec((TM_FFN, d), lambda i, j: (i, 0)),
        out_shape=jax.ShapeDtypeStruct((t, d), F32),
        scratch_shapes=[pltpu.VMEM((TM_FFN, d), BF16), pltpu.VMEM((TM_FFN, LANES), F32)],
        compiler_params=_params(("parallel", "arbitrary")),
        name="ffn",
    )(x2, g.reshape(1, d), scale, shift, gate, w_in, w_in, w_out)


def _rope_lanes(a, axis):
    x1, x2 = jnp.split(a, 2, axis=axis)
    zero = jnp.zeros_like(x1)
    return jnp.concatenate([x1, zero, x2, zero], axis=axis)


def _in_offsets():
    offs, off = [], 0
    for size in IN_SIZES:
        offs.append(off)
        off += size
    return offs


(IN_QM, IN_KM, IN_VM, IN_OM, IN_IM, IN_FM, IN_QLAT, IN_KVLAT, IN_KR, IN_POOL, IN_GATES) = _in_offsets()
IN_W = sum(IN_SIZES)

_Z_COPIES = ((Z_GATES, IN_GATES, N_BRANCH * D_MODEL), (Z_QM, IN_QM, M_QK_W), (Z_QLAT, IN_QLAT, Q_RANK),
             (Z_VM, IN_VM, M_V_W), (Z_OM, IN_OM, M_V_W), (Z_KVLAT, IN_KVLAT, KV_RANK),
             (Z_POOL, IN_POOL, POOL_W))
PREP_COLS = 256
PREP_ROWS = 512


def _w_prep_kernel(w_ref, o_ref, kt_ref, if_ref):
    cols = w_ref.shape[1]
    kt_ref[...] = w_ref[IN_KM:IN_KM + M_QK_W, :].astype(BF16)
    if_ref[...] = jnp.concatenate(
        [w_ref[IN_IM:IN_IM + 2 * M_HEADS, :], jnp.zeros((GATE_ROWS - 2 * M_HEADS, cols), F32)],
        axis=0).astype(BF16)
    for dst, src, width in _Z_COPIES:
        for r in range(0, width, PREP_ROWS):
            n = min(PREP_ROWS, width - r)
            o_ref[dst + r:dst + r + n, :] = w_ref[src + r:src + r + n, :].astype(BF16)
    zero = jnp.zeros((ROPE_HALF, cols), BF16)
    x1 = w_ref[IN_KR:IN_KR + ROPE_HALF, :].astype(BF16)
    x2 = w_ref[IN_KR + ROPE_HALF:IN_KR + ROPE_DIM, :].astype(BF16)
    for k, part in enumerate((x1, zero, x2, zero, x2, zero, x1, zero)):
        o_ref[Z_KR + k * ROPE_HALF:Z_KR + (k + 1) * ROPE_HALF, :] = part


def _prep_w_in(w_in):
    depth, d, _ = w_in.shape
    return pl.pallas_call(
        _w_prep_kernel,
        grid=(depth, d // PREP_COLS),
        in_specs=[pl.BlockSpec((None, IN_W, PREP_COLS), lambda l, c: (l, 0, c))],
        out_specs=[
            pl.BlockSpec((None, Z_W, PREP_COLS), lambda l, c: (l, 0, c)),
            pl.BlockSpec((None, M_QK_W, PREP_COLS), lambda l, c: (l, 0, c)),
            pl.BlockSpec((None, GATE_ROWS, PREP_COLS), lambda l, c: (l, 0, c)),
        ],
        out_shape=[
            jax.ShapeDtypeStruct((depth, Z_W, d), BF16),
            jax.ShapeDtypeStruct((depth, M_QK_W, d), BF16),
            jax.ShapeDtypeStruct((depth, GATE_ROWS, d), BF16),
        ],
        compiler_params=_params(("parallel", "parallel")),
        name="w_prep",
    )(jnp.swapaxes(w_in, 1, 2))


def _prep_head_cols(a):
    return jnp.concatenate([a[..., :NOPE_DIM], _rope_lanes(a[..., NOPE_DIM:], -1)], axis=-1)


def _prep_rope_swapped(a):
    return _rope_lanes(
        jnp.concatenate([a[..., NOPE_DIM + ROPE_HALF:], a[..., NOPE_DIM:NOPE_DIM + ROPE_HALF]],
                        axis=-1), -1)


def _rope_tables(seq):
    pos = jnp.arange(seq, dtype=F32)
    freqs = ROPE_THETA ** (-jnp.arange(0, ROPE_DIM, 2, dtype=F32) / ROPE_DIM)
    ang = pos[:, None] * freqs[None, :]
    cos, sin = jnp.cos(ang), jnp.sin(ang)
    zero = jnp.zeros_like(cos)
    cos_t = jnp.concatenate([cos, zero, cos, zero], axis=1)
    sin_t = jnp.concatenate([-sin, zero, sin, zero], axis=1)
    return cos_t, sin_t


def kernel(x, c, w_ada, b_ada, g_norm1, w_in, b_mgate, g_mnorm, g_qlat, w_uq, g_kvlat, w_ukv,
           g_qn, g_kn, w_pool, b_pool, s_pool, w_branch, w_out, g_norm2, w_ffn_in, w_ffn_out):
    batch, seq, d = x.shape
    depth = w_in.shape[0]
    t = batch * seq

    mod = _ada(c, w_ada, b_ada)
    cos_t, sin_t = _rope_tables(seq)
    x2 = x.reshape(t, d)
    w_pool_b, w_branch_b, w_out_b = w_pool.astype(BF16), w_branch.astype(BF16), w_out.astype(BF16)
    w_ffn_in_b, w_ffn_out_b = w_ffn_in.astype(BF16), w_ffn_out.astype(BF16)
    w_z, w_kt, w_if = _prep_w_in(w_in)

    for l in range(depth):
        shift1, scale1, gate1, shift2, scale2, gate2 = [
            mod[l, :, k * d:(k + 1) * d].reshape(batch, 1, d) for k in range(6)]

        z, kt, gates_t = _inproj(x2, g_norm1[l], scale1, shift1, w_z, l, w_kt, w_if, seq)

        br_a = _mlstm(z.reshape(batch, seq, Z_W), kt, gates_t, b_mgate[l],
                      g_mnorm[l]).reshape(t, M_V_W)

        w_uq_h = w_uq[l].reshape(Q_RANK, MLA_HEADS, QK_DIM)
        w_q = _prep_head_cols(w_uq_h).reshape(Q_RANK, MLA_HEADS * QK_PAD).astype(BF16)
        w_qs = _prep_rope_swapped(w_uq_h).reshape(Q_RANK, MLA_HEADS * LANES).astype(BF16)
        w_kv = w_ukv[l].reshape(KV_RANK, MLA_HEADS, NOPE_DIM + MLA_V)
        w_k = w_kv[..., :NOPE_DIM].reshape(KV_RANK, MLA_HEADS * NOPE_DIM).astype(BF16)
        w_v = w_kv[..., NOPE_DIM:].reshape(KV_RANK, MLA_W).astype(BF16)
        q, k, v = _mla_proj(z, g_qlat[l], g_kvlat[l], w_q, w_qs, w_k, w_v,
                            _prep_head_cols(g_qn[l]).reshape(1, QK_PAD),
                            _prep_rope_swapped(g_qn[l]).reshape(1, LANES),
                            _prep_head_cols(g_kn[l]).reshape(1, QK_PAD),
                            _prep_rope_swapped(g_kn[l]).reshape(1, LANES), cos_t, sin_t, seq)
        br_b = _attention(q, k, v, batch, seq)

        br_c = _pool(z, w_pool_b, l, b_pool[l], s_pool[l], batch, seq)

        x2 = _merge(br_a, br_b, br_c, z, w_branch_b, w_out_b, l, x2, gate1, seq)
        x2 = _ffn(x2, g_norm2[l], scale2, shift2, gate2, w_ffn_in_b, w_ffn_out_b, l, seq)

    return x2.reshape(batch, seq, d)
```
